```python
import math
import jax, jax.numpy as jnp
from jax import lax
import numpy as np

D_MODEL = 1024
BATCH = 16
SEQ = 256
DEPTH = 1
DEC_BATCH = 4
DEC_SEQ = 4096
PAST_LEN = 256

GRID_W = 64
ROPE_BASE = 10000.0
RET_HEADS = 4
RET_DK = 128
RET_DV = 256
RET_CHUNK = 128
DIFF_HEADS = 8
DIFF_DH = 64
Q_BLOCK = 128
FFN_HIDDEN = -(-8 * D_MODEL // (3 * 256)) * 256
RET_QK_W = RET_HEADS * RET_DK
RET_V_W = RET_HEADS * RET_DV
DIFF_QK_W = DIFF_HEADS * 2 * DIFF_DH
DIFF_V_W = DIFF_HEADS * 2 * DIFF_DH
W_IN_COLS = 2 * RET_QK_W + 2 * RET_V_W + 2 * DIFF_QK_W + DIFF_V_W + 2 * D_MODEL
RMS_EPS = 1e-6
GN_EPS = 1e-5

kernel_name = "hybrid_retention_diffattn_dit_step"


def rmsnorm(x, g):
    xf = x.astype(jnp.float32)
    y = xf * lax.rsqrt(jnp.mean(xf * xf, axis=-1, keepdims=True) + RMS_EPS)
    return (y * g).astype(x.dtype)


def head_group_norm(y, g):
    yf = y.astype(jnp.float32)
    mu = jnp.mean(yf, axis=-1, keepdims=True)
    var = jnp.mean(jnp.square(yf - mu), axis=-1, keepdims=True)
    return ((yf - mu) * lax.rsqrt(var + GN_EPS) * g).astype(y.dtype)


def modulation(cvec, w_ada, b_ada):
    return jax.nn.silu(cvec) @ w_ada + b_ada


def axial_rope_tables(L, d):
    rows = L // GRID_W
    pos_r = jnp.repeat(jnp.arange(rows), GRID_W).astype(jnp.float32)
    pos_c = jnp.tile(jnp.arange(GRID_W), rows).astype(jnp.float32)
    n = d // 4
    inv = ROPE_BASE ** (-jnp.arange(n, dtype=jnp.float32) / n)
    ang = jnp.concatenate([pos_r[:, None] * inv, pos_c[:, None] * inv], axis=-1)
    return jnp.cos(ang), jnp.sin(ang)


def apply_rope(x, cos, sin):
    shape = (x.shape[1],) + (1,) * (x.ndim - 3) + (cos.shape[-1],)
    c = cos.reshape(shape)
    s = sin.reshape(shape)
    x1, x2 = jnp.split(x, 2, axis=-1)
    return jnp.concatenate([x1 * c - x2 * s, x1 * s + x2 * c], axis=-1).astype(x.dtype)


def retention_scan(q, k, v, log_gamma, s0, inclusive):
    B, L, H, dk = q.shape
    dv = v.shape[-1]
    C = RET_CHUNK
    N = L // C
    qc = q.reshape(B, N, C, H, dk)
    kc = k.reshape(B, N, C, H, dk)
    vc = v.reshape(B, N, C, H, dv)
    idx = jnp.arange(C, dtype=jnp.float32)
    dist = idx[:, None] - idx[None, :]
    mask = (dist >= 0) if inclusive else (dist > 0)
    decay = jnp.where(mask[None], jnp.exp(log_gamma[:, None, None] * jnp.where(mask, dist, 0.0)[None]), 0.0)
    scores = jnp.einsum('bnihd,bnjhd->bnhij', qc, kc) * decay[None, None]
    intra = jnp.einsum('bnhij,bnjhe->bnihe', scores, vc)
    w_k = jnp.exp(log_gamma[:, None] * (C - 1.0 - idx)[None])
    kv = jnp.einsum('bnjhd,hj,bnjhe->bnhde', kc, w_k, vc).astype(jnp.float32)
    gamma_c = jnp.exp(log_gamma * C)[None, :, None, None]

    def step(s, kv_n):
        return gamma_c * s + kv_n, s

    s_final, s_prev = lax.scan(step, s0.astype(jnp.float32), jnp.moveaxis(kv, 1, 0))
    s_prev = jnp.moveaxis(s_prev, 0, 1)
    w_q = jnp.exp(log_gamma[:, None] * (idx + 1.0)[None])
    cross = jnp.einsum('bnihd,hi,bnhde->bnihe', qc, w_q, s_prev)
    out = (intra + cross).reshape(B, L, H, dv).astype(v.dtype)
    return out, s_final.astype(v.dtype)


def diff_attention(q, k, v, lam):
    B, Lq, H, _, dh = q.shape
    nb = Lq // Q_BLOCK
    qb = jnp.moveaxis(q.reshape(B, nb, Q_BLOCK, H, 2, dh), 1, 0)
    scale = dh ** -0.5

    def block(qblk):
        s = jnp.einsum('bqhmd,bkhmd->bhmqk', qblk, k, preferred_element_type=jnp.float32) * scale
        p = jax.nn.softmax(s, axis=-1)
        a = p[:, :, 0] - lam * p[:, :, 1]
        return jnp.einsum('bhqk,bkhe->bqhe', a.astype(v.dtype), v)

    out = lax.map(block, qb)
    return jnp.moveaxis(out, 0, 1).reshape(B, Lq, H, v.shape[-1])


def split_projection(proj):
    widths = (RET_QK_W, RET_QK_W, RET_V_W, RET_V_W, DIFF_QK_W, DIFF_QK_W, DIFF_V_W)
    points, acc = [], 0
    for w in widths:
        acc += w
        points.append(acc)
    return jnp.split(proj, points, axis=-1)


def trunk_layer(x, mod, norm1_g, norm2_g, w_in, b_gate, ret_decay_fwd, ret_decay_bwd, ret_gn_g,
                w_ret_out, lam, lam_init, diff_subln_g, w_diff_out, w_o, w_ffn_gate, w_ffn_up,
                w_ffn_down, ctx_cache):
    B, L, _ = x.shape
    latent = ctx_cache is not None
    shift1, scale1, gate1, shift2, scale2, gate2 = jnp.split(mod, 6, axis=-1)
    h = rmsnorm(x, norm1_g) * (1.0 + scale1) + shift1
    proj = h @ w_in
    q_r, k_r, v_r, g_r, q_d, k_d, v_d, gates = split_projection(proj)
    gates = jax.nn.sigmoid(gates + b_gate)
    gate_r, gate_d = jnp.split(gates, 2, axis=-1)

    q_r = q_r.reshape(B, L, RET_HEADS, RET_DK)
    k_r = k_r.reshape(B, L, RET_HEADS, RET_DK)
    v_r = v_r.reshape(B, L, RET_HEADS, RET_DV)
    q_d = q_d.reshape(B, L, DIFF_HEADS, 2, DIFF_DH)
    k_d = k_d.reshape(B, L, DIFF_HEADS, 2, DIFF_DH)
    v_d = v_d.reshape(B, L, DIFF_HEADS, 2 * DIFF_DH)

    if latent:
        cos_r, sin_r = axial_rope_tables(L, RET_DK)
        q_r = apply_rope(q_r, cos_r, sin_r)
        k_r = apply_rope(k_r, cos_r, sin_r)
        cos_d, sin_d = axial_rope_tables(L, DIFF_DH)
        q_d = apply_rope(q_d, cos_d, sin_d)
        k_d = apply_rope(k_d, cos_d, sin_d)
        s0_f, s0_b, k_ctx, v_ctx = ctx_cache
    else:
        s0_f = jnp.zeros((B, RET_HEADS, RET_DK, RET_DV), jnp.float32)
        s0_b = s0_f
    k_r = k_r * (RET_DK ** -0.5)

    lg_f = jax.nn.log_sigmoid(ret_decay_fwd.astype(jnp.float32))
    lg_b = jax.nn.log_sigmoid(ret_decay_bwd.astype(jnp.float32))
    out_f, s_f = retention_scan(q_r, k_r, v_r, lg_f, s0_f, True)
    out_b_rev, s_b = retention_scan(jnp.flip(q_r, 1), jnp.flip(k_r, 1), jnp.flip(v_r, 1), lg_b, s0_b, False)
    y_r = head_group_norm(out_f + jnp.flip(out_b_rev, 1), ret_gn_g.reshape(RET_HEADS, RET_DV))
    branch_r = (jax.nn.silu(g_r) * y_r.reshape(B, L, RET_V_W)) @ w_ret_out

    if latent:
        k_all = jnp.concatenate([k_d, k_ctx.astype(k_d.dtype)], axis=1)
        v_all = jnp.concatenate([v_d, v_ctx.astype(v_d.dtype)], axis=1)
    else:
        k_all, v_all = k_d, v_d
    o_d = diff_attention(q_d, k_all, v_all, lam)
    o_d = rmsnorm(o_d, diff_subln_g) * (1.0 - lam_init)
    branch_d = o_d.reshape(B, L, DIFF_V_W) @ w_diff_out

    mix = (gate_r * branch_r + gate_d * branch_d) @ w_o
    x = x + gate1 * mix
    h2 = rmsnorm(x, norm2_g) * (1.0 + scale2) + shift2
    ffn = (jax.nn.silu(h2 @ w_ffn_gate) * (h2 @ w_ffn_up)) @ w_ffn_down
    x = x + gate2 * ffn
    ctx_tensors = None if latent else (s_f, s_b, k_d, v_d)
    return x, ctx_tensors


def setup_inputs(seed: int = 0) -> dict:
    key = jax.random.key(seed)
    ks = jax.random.split(key, 32)
    f32 = jnp.float32
    D, F = D_MODEL, FFN_HIDDEN

    def nrm(k, shape, scale):
        return jax.random.normal(k, shape, f32) * scale

    base = 1.0 - 2.0 ** (-5.0 - jnp.arange(RET_HEADS, dtype=f32))
    decay_logit = jnp.log(base) - jnp.log1p(-base)
    return {
        "x_prompt": nrm(ks[0], (BATCH, SEQ, D), 1.0),
        "x_sample": nrm(ks[1], (DEC_BATCH, DEC_SEQ, D), 1.0),
        "state_ret_fwd": nrm(ks[2], (DEC_BATCH, DEPTH, RET_HEADS, RET_DK, RET_DV), 1.0),
        "state_ret_bwd": nrm(ks[3], (DEC_BATCH, DEPTH, RET_HEADS, RET_DK, RET_DV), 1.0),
        "cache_diff_k": nrm(ks[4], (DEC_BATCH, DEPTH, PAST_LEN, DIFF_HEADS, 2, DIFF_DH), 1.0),
        "cache_diff_v": nrm(ks[5], (DEC_BATCH, DEPTH, PAST_LEN, DIFF_HEADS, 2 * DIFF_DH), 1.0),
        "c": nrm(ks[6], (DEC_BATCH, D), 1.0),
        "c_ctx": nrm(ks[7], (D,), 1.0),
        "norm1_g": 1.0 + nrm(ks[8], (DEPTH, D), 0.02),
        "norm2_g": 1.0 + nrm(ks[9], (DEPTH, D), 0.02),
        "w_ada": nrm(ks[10], (DEPTH, D, 6 * D), 0.5 * D ** -0.5),
        "b_ada": nrm(ks[11], (DEPTH, 6 * D), 0.02),
        "w_in": nrm(ks[12], (DEPTH, D, W_IN_COLS), D ** -0.5),
        "b_gate": nrm(ks[13], (DEPTH, 2 * D), 0.1),
        "ret_decay_fwd": decay_logit[None] + nrm(ks[14], (DEPTH, RET_HEADS), 0.1),
        "ret_decay_bwd": decay_logit[None] + nrm(ks[15], (DEPTH, RET_HEADS), 0.1),
        "ret_gn_g": 1.0 + nrm(ks[16], (DEPTH, RET_V_W), 0.02),
        "w_ret_out": nrm(ks[17], (DEPTH, RET_V_W, D), RET_V_W ** -0.5),
        "diff_lambda_q1": nrm(ks[18], (DEPTH, DIFF_DH), 0.1),
        "diff_lambda_k1": nrm(ks[19], (DEPTH, DIFF_DH), 0.1),
        "diff_lambda_q2": nrm(ks[20], (DEPTH, DIFF_DH), 0.1),
        "diff_lambda_k2": nrm(ks[21], (DEPTH, DIFF_DH), 0.1),
        "diff_subln_g": 1.0 + nrm(ks[22], (DEPTH, 2 * DIFF_DH), 0.02),
        "w_diff_out": nrm(ks[23], (DEPTH, DIFF_V_W, D), DIFF_V_W ** -0.5),
        "w_o": nrm(ks[24], (DEPTH, D, D), D ** -0.5),
        "w_ffn_gate": nrm(ks[25], (DEPTH, D, F), D ** -0.5),
        "w_ffn_up": nrm(ks[26], (DEPTH, D, F), D ** -0.5),
        "w_ffn_down": nrm(ks[27], (DEPTH, F, D), F ** -0.5),
        "final_norm_g": 1.0 + nrm(ks[28], (D,), 0.02),
    }


def reference(x_prompt, x_sample, state_ret_fwd, state_ret_bwd, cache_diff_k, cache_diff_v, c,
              c_ctx, norm1_g, norm2_g, w_ada, b_ada, w_in, b_gate, ret_decay_fwd, ret_decay_bwd,
              ret_gn_g, w_ret_out, diff_lambda_q1, diff_lambda_k1, diff_lambda_q2, diff_lambda_k2,
              diff_subln_g, w_diff_out, w_o, w_ffn_gate, w_ffn_up, w_ffn_down, final_norm_g):
    xp, xs = x_prompt, x_sample
    new_f, new_b, new_k, new_v = [], [], [], []
    for l in range(DEPTH):
        lam_init = 0.8 - 0.6 * math.exp(-0.3 * l)
        lam = (jnp.exp(jnp.sum(diff_lambda_q1[l].astype(jnp.float32) * diff_lambda_k1[l].astype(jnp.float32)))
               - jnp.exp(jnp.sum(diff_lambda_q2[l].astype(jnp.float32) * diff_lambda_k2[l].astype(jnp.float32)))
               + lam_init)
        mod_ctx = modulation(c_ctx, w_ada[l], b_ada[l])[None, None, :]
        mod_lat = modulation(c, w_ada[l], b_ada[l])[:, None, :]
        weights = (norm1_g[l], norm2_g[l], w_in[l], b_gate[l], ret_decay_fwd[l], ret_decay_bwd[l],
                   ret_gn_g[l], w_ret_out[l], lam, lam_init, diff_subln_g[l], w_diff_out[l], w_o[l],
                   w_ffn_gate[l], w_ffn_up[l], w_ffn_down[l])
        xp, ctx_t = trunk_layer(xp, mod_ctx, *weights, None)
        xs, _ = trunk_layer(xs, mod_lat, *weights,
                            (state_ret_fwd[:, l], state_ret_bwd[:, l], cache_diff_k[:, l], cache_diff_v[:, l]))
        new_f.append(ctx_t[0])
        new_b.append(ctx_t[1])
        new_k.append(ctx_t[2])
        new_v.append(ctx_t[3])
    y_prompt = rmsnorm(xp, final_norm_g)
    y_sample = rmsnorm(xs, final_norm_g)
    new_ret_fwd = jnp.stack(new_f, axis=1)
    new_ret_bwd = jnp.stack(new_b, axis=1)
    new_diff_k = jnp.stack(new_k, axis=1)
    new_diff_v = jnp.stack(new_v, axis=1)
    return (y_prompt, y_sample, new_ret_fwd, new_ret_bwd, new_diff_k, new_diff_v)
```

```python
import functools
import math

import jax
import jax.numpy as jnp
from jax import lax
from jax.experimental import pallas as pl
from jax.experimental.pallas import tpu as pltpu

D_MODEL = 1024
GRID_W = 64
ROPE_BASE = 10000.0
RET_HEADS = 4
RET_DK = 128
RET_DV = 256
RET_CHUNK = 128
DIFF_HEADS = 8
DIFF_DH = 64
FFN_HIDDEN = 2816
W_IN_COLS = 8192
RMS_EPS = 1e-6
GN_EPS = 1e-5
LAM_INIT = 0.8 - 0.6 * math.exp(-0.3 * 0)

SEG = 1024
N_SEG = W_IN_COLS // SEG
LANES = 128
MOD_ROWS = 8
CTX_ROW = 4
K_R_SCALE = RET_DK ** -0.5
Q_D_SCALE = DIFF_DH ** -0.5
FFN_SPLITS = ((0, 1536), (1536, FFN_HIDDEN))

BF = jnp.bfloat16
F32 = jnp.float32
VMEM_LIMIT = 56 * 1024 * 1024


def _params(n_axes):
    return pltpu.CompilerParams(dimension_semantics=("arbitrary",) * n_axes,
                                vmem_limit_bytes=VMEM_LIMIT)


def _mod_kernel(c_ref, w_ref, b_ref, o_ref):
    c = c_ref[...]
    s = c * jax.nn.sigmoid(c)
    o_ref[...] = jnp.dot(s, w_ref[...], preferred_element_type=F32,
                         precision=lax.Precision.HIGHEST) + b_ref[...]


def _modulation(cvec, w_ada, b_ada):
    n_out = w_ada.shape[1]
    tn = 1024
    return pl.pallas_call(
        _mod_kernel,
        out_shape=jax.ShapeDtypeStruct((MOD_ROWS, n_out), F32),
        grid=(n_out // tn,),
        in_specs=[pl.BlockSpec((MOD_ROWS, D_MODEL), lambda n: (0, 0)),
                  pl.BlockSpec((D_MODEL, tn), lambda n: (0, n)),
                  pl.BlockSpec((1, tn), lambda n: (0, n))],
        out_specs=pl.BlockSpec((MOD_ROWS, tn), lambda n: (0, n)),
        compiler_params=_params(1),
        name="modulation",
    )(cvec, w_ada, b_ada)


def _rope_tables(seq_len):
    rows = seq_len // GRID_W
    pos_r = jnp.repeat(jnp.arange(rows), GRID_W).astype(F32)
    pos_c = jnp.tile(jnp.arange(GRID_W), rows).astype(F32)

    def angles(d):
        n = d // 4
        inv = ROPE_BASE ** (-jnp.arange(n, dtype=F32) / n)
        return jnp.concatenate([pos_r[:, None] * inv, pos_c[:, None] * inv], axis=-1)

    ang_r = angles(RET_DK)
    cos_r = jnp.concatenate([jnp.cos(ang_r)] * 2, axis=-1)
    sin_r = jnp.concatenate([-jnp.sin(ang_r), jnp.sin(ang_r)], axis=-1)
    ang_d = angles(DIFF_DH)
    zero = jnp.zeros_like(ang_d)
    cos_d = jnp.concatenate([jnp.cos(ang_d)] * 4, axis=-1)
    sin_lo = jnp.concatenate([zero, jnp.sin(ang_d)] * 2, axis=-1)
    sin_hi = jnp.concatenate([-jnp.sin(ang_d), zero] * 2, axis=-1)
    return cos_r, sin_r, cos_d, sin_lo, sin_hi


def _inproj_kernel(*refs, latent):
    if latent:
        (x_ref, sh_ref, sc_ref, g1_ref, w_ref, bg_ref,
         cr_ref, sr_ref, cd_ref, slo_ref, shi_ref, o_ref, h_scr) = refs
    else:
        x_ref, sh_ref, sc_ref, g1_ref, w_ref, bg_ref, o_ref, ko_ref, vo_ref, h_scr = refs
    n = pl.program_id(2)

    @pl.when(n == 0)
    def _():
        x = x_ref[0]
        ms = jnp.mean(x * x, axis=-1, keepdims=True)
        y = x * lax.rsqrt(ms + RMS_EPS) * g1_ref[...]
        h_scr[...] = (y * (1.0 + sc_ref[0]) + sh_ref[0]).astype(BF)

    acc = jnp.dot(h_scr[...], w_ref[...], preferred_element_type=F32)
    heads = SEG // LANES

    def rope_r(a):
        return a * cr_ref[...] + pltpu.roll(a, 64, 1) * sr_ref[...]

    def rope_d(a):
        return a * cd_ref[...] + pltpu.roll(a, 32, 1) * slo_ref[...] + pltpu.roll(a, 96, 1) * shi_ref[...]

    @pl.when(n == 0)
    def _():
        for j in range(heads):
            a = acc[:, j * LANES:(j + 1) * LANES]
            if latent:
                a = rope_r(a)
            if j >= heads // 2:
                a = a * K_R_SCALE
            o_ref[0, :, j * LANES:(j + 1) * LANES] = a.astype(BF)

    @pl.when(n == 1)
    def _():
        o_ref[0] = acc.astype(BF)

    @pl.when(n == 2)
    def _():
        o_ref[0] = (acc * jax.nn.sigmoid(acc)).astype(BF)

    @pl.when(n == 3)
    def _():
        for j in range(heads):
            a = acc[:, j * LANES:(j + 1) * LANES]
            if latent:
                a = rope_d(a)
            o_ref[0, :, j * LANES:(j + 1) * LANES] = (a * Q_D_SCALE).astype(BF)

    @pl.when(n == 4)
    def _():
        if latent:
            for j in range(heads):
                a = rope_d(acc[:, j * LANES:(j + 1) * LANES])
                o_ref[0, :, j * LANES:(j + 1) * LANES] = a.astype(BF)
        else:
            o_ref[0] = acc.astype(BF)
            ko_ref[0] = acc

    @pl.when(n == 5)
    def _():
        o_ref[0] = acc.astype(BF)
        if not latent:
            vo_ref[0] = acc

    @pl.when(n >= 6)
    def _():
        o_ref[0] = jax.nn.sigmoid(acc + bg_ref[...]).astype(BF)


def _inproj(x, mod3, norm_g, w_in_bf, b_gate, rope, *, latent, tm):
    bx, lx, _ = x.shape
    row = (lambda b: b) if latent else (lambda b: CTX_ROW)

    def mod_spec(k):
        return pl.BlockSpec((1, 1, D_MODEL), lambda b, m, n: (row(b), 0, k))

    in_specs = [pl.BlockSpec((1, tm, D_MODEL), lambda b, m, n: (b, m, 0)),
                mod_spec(0), mod_spec(1),
                pl.BlockSpec((1, D_MODEL), lambda b, m, n: (0, 0)),
                pl.BlockSpec((D_MODEL, SEG), lambda b, m, n: (0, n)),
                pl.BlockSpec((1, SEG), lambda b, m, n: (0, jnp.maximum(n - 6, 0)))]
    args = [x, mod3, mod3, norm_g, w_in_bf, b_gate]
    proj_shape = jax.ShapeDtypeStruct((bx, lx, W_IN_COLS), BF)
    proj_spec = pl.BlockSpec((1, tm, SEG), lambda b, m, n: (b, m, n))
    if latent:
        in_specs += [pl.BlockSpec((tm, LANES), lambda b, m, n: (m, 0))] * 5
        args += list(rope)
        out_shape, out_specs = proj_shape, proj_spec
    else:
        kv_shape = jax.ShapeDtypeStruct((bx, lx, SEG), F32)
        kv_spec = pl.BlockSpec((1, tm, SEG), lambda b, m, n: (b, m, 0))
        out_shape, out_specs = (proj_shape, kv_shape, kv_shape), (proj_spec, kv_spec, kv_spec)
    return pl.pallas_call(
        functools.partial(_inproj_kernel, latent=latent),
        out_shape=out_shape,
        grid=(bx, lx // tm, N_SEG),
        in_specs=in_specs,
        out_specs=out_specs,
        scratch_shapes=[pltpu.VMEM((tm, D_MODEL), BF)],
        compiler_params=_params(3),
        name="inproj_latent" if latent else "inproj_ctx",
    )(*args)


def _ret_kernel(*refs, n_chunks, has_s0, emit_state):
    it = iter(refs)
    q_ref, k_ref, v_ref, g_ref, df_ref, db_ref, gn_ref = (next(it) for _ in range(7))
    s0f_ref, s0b_ref = (next(it), next(it)) if has_s0 else (None, None)
    y_ref = next(it)
    sfo_ref, sbo_ref = (next(it), next(it)) if emit_state else (None, None)
    sf_scr, sb_scr = next(it), next(it)
    C = RET_CHUNK
    h = pl.program_id(1)
    lane = lax.broadcasted_iota(jnp.int32, (1, RET_HEADS), 1)

    def head_log_gamma(ref):
        lg = jax.nn.log_sigmoid(ref[...].astype(F32))
        return jnp.sum(jnp.where(lane == h, lg, 0.0), axis=-1, keepdims=True)

    lgf, lgb = head_log_gamma(df_ref), head_log_gamma(db_ref)
    ri = lax.broadcasted_iota(jnp.int32, (C, C), 0).astype(F32)
    ci = lax.broadcasted_iota(jnp.int32, (C, C), 1).astype(F32)
    dist = ri - ci
    decay = jnp.where(dist >= 0, jnp.exp(lgf * jnp.maximum(dist, 0.0)), jnp.exp(lgb * jnp.maximum(-dist, 0.0)))
    rk = lax.broadcasted_iota(jnp.int32, (C, RET_DK), 0).astype(F32)
    wq_f = jnp.exp(lgf * (rk + 1.0))
    wq_b = jnp.exp(lgb * (C - rk))
    wk_f = jnp.exp(lgf * (C - 1.0 - rk))
    wk_b = jnp.exp(lgb * rk)
    gc_f = jnp.exp(lgf * C)
    gc_b = jnp.exp(lgb * C)

    def rows(c):
        return pl.ds(pl.multiple_of(c * C, C), C)

    def kv_update(c, state, wk, gc):
        kw = (k_ref[0, rows(c), :].astype(F32) * wk).astype(BF)
        kv = lax.dot_general(kw, v_ref[0, rows(c), :], (((0,), (0,)), ((), ())),
                             preferred_element_type=F32)
        return gc * state + kv

    def scan_body(n, carry):
        sf, sb = carry
        cb = n_chunks - 1 - n
        sf_scr[n] = sf.astype(BF)
        sb_scr[cb] = sb.astype(BF)
        return kv_update(n, sf, wk_f, gc_f), kv_update(cb, sb, wk_b, gc_b)

    if has_s0:
        init = (s0f_ref[...].astype(F32), s0b_ref[...].astype(F32))
    else:
        init = (jnp.zeros((RET_DK, RET_DV), F32), jnp.zeros((RET_DK, RET_DV), F32))
    sf, sb = lax.fori_loop(0, n_chunks, scan_body, init)
    if emit_state:
        sfo_ref[...] = sf
        sbo_ref[...] = sb

    gn = gn_ref[...]

    def out_body(c, _):
        q = q_ref[0, rows(c), :]
        qf32 = q.astype(F32)
        sc = lax.dot_general(q, k_ref[0, rows(c), :], (((1,), (1,)), ((), ())),
                             preferred_element_type=F32)
        o = jnp.dot((sc * decay).astype(BF), v_ref[0, rows(c), :], preferred_element_type=F32)
        o = o + jnp.dot((qf32 * wq_f).astype(BF), sf_scr[c], preferred_element_type=F32)
        o = o + jnp.dot((qf32 * wq_b).astype(BF), sb_scr[c], preferred_element_type=F32)
        mu = jnp.mean(o, axis=-1, keepdims=True)
        oc = o - mu
        var = jnp.mean(oc * oc, axis=-1, keepdims=True)
        y = oc * lax.rsqrt(var + GN_EPS) * gn
        y_ref[0, rows(c), :] = (g_ref[0, rows(c), :].astype(F32) * y).astype(BF)
        return 0

    lax.fori_loop(0, n_chunks, out_body, 0)


def _retention(proj, decay_f, decay_b, gn_g, s0_f, s0_b, *, emit_state):
    bx, lx, _ = proj.shape
    n_chunks = lx // RET_CHUNK
    has_s0 = s0_f is not None
    qk_off = (RET_HEADS * RET_DK) // RET_DK
    v_off = (2 * RET_HEADS * RET_DK) // RET_DV
    g_off = v_off + RET_HEADS
    state_spec = pl.BlockSpec((None, None, None, RET_DK, RET_DV), lambda b, h: (b, 0, h, 0, 0))
    in_specs = [pl.BlockSpec((1, lx, RET_DK), lambda b, h: (b, 0, h)),
                pl.BlockSpec((1, lx, RET_DK), lambda b, h: (b, 0, qk_off + h)),
                pl.BlockSpec((1, lx, RET_DV), lambda b, h: (b, 0, v_off + h)),
                pl.BlockSpec((1, lx, RET_DV), lambda b, h: (b, 0, g_off + h)),
                pl.BlockSpec((1, RET_HEADS), lambda b, h: (0, 0)),
                pl.BlockSpec((1, RET_HEADS), lambda b, h: (0, 0)),
                pl.BlockSpec((1, RET_DV), lambda b, h: (0, h))]
    args = [proj, proj, proj, proj, decay_f, decay_b, gn_g]
    if has_s0:
        in_specs += [state_spec, state_spec]
        args += [s0_f, s0_b]
    y_shape = jax.ShapeDtypeStruct((bx, lx, RET_HEADS * RET_DV), BF)
    y_spec = pl.BlockSpec((1, lx, RET_DV), lambda b, h: (b, 0, h))
    if emit_state:
        st_shape = jax.ShapeDtypeStruct((bx, 1, RET_HEADS, RET_DK, RET_DV), F32)
        out_shape, out_specs = (y_shape, st_shape, st_shape), (y_spec, state_spec, state_spec)
    else:
        out_shape, out_specs = y_shape, y_spec
    snap = pltpu.VMEM((n_chunks, RET_DK, RET_DV), BF)
    return pl.pallas_call(
        functools.partial(_ret_kernel, n_chunks=n_chunks, has_s0=has_s0, emit_state=emit_state),
        out_shape=out_shape,
        grid=(bx, RET_HEADS),
        in_specs=in_specs,
        out_specs=out_specs,
        scratch_shapes=[snap, snap],
        compiler_params=_params(2),
        name="retention_latent" if has_s0 else "retention_ctx",
    )(*args)


def _attn_kernel(*refs, lk_self, lk_ctx, bq, kc):
    it = iter(refs)
    q_ref, k_ref, v_ref = next(it), next(it), next(it)
    kctx_ref, vctx_ref = (next(it), next(it)) if lk_ctx else (None, None)
    lq1_ref, lk1_ref, lq2_ref, lk2_ref, sg_ref = (next(it) for _ in range(5))
    o_ref = next(it)
    kall, vall, s_scr = next(it), next(it), next(it)
    lk = lk_self + lk_ctx
    n_kc = lk // kc

    @pl.when(pl.program_id(2) == 0)
    def _():
        kall[0:lk_self, :] = k_ref[0]
        vall[0:lk_self, :] = v_ref[0]
        if lk_ctx:
            kall[lk_self:lk, :] = kctx_ref[0].astype(BF)
            vall[lk_self:lk, :] = vctx_ref[0].astype(BF)

    def lam_term(a_ref, b_ref):
        return jnp.exp(jnp.sum(a_ref[...].astype(F32) * b_ref[...].astype(F32), axis=-1, keepdims=True))

    lam = lam_term(lq1_ref, lk1_ref) - lam_term(lq2_ref, lk2_ref) + LAM_INIT

    q = q_ref[0]
    lane = lax.broadcasted_iota(jnp.int32, (bq, LANES), 1)
    zero = jnp.zeros_like(q)
    q_maps = (jnp.where(lane < DIFF_DH, q, zero), jnp.where(lane >= DIFF_DH, q, zero))

    m = [jnp.full((bq, 1), -jnp.inf, F32) for _ in range(2)]
    for c in range(n_kc):
        kk = kall[c * kc:(c + 1) * kc, :]
        for i in range(2):
            s = lax.dot_general(q_maps[i], kk, (((1,), (1,)), ((), ())), preferred_element_type=F32)
            s_scr[i, :, c * kc:(c + 1) * kc] = s
            m[i] = jnp.maximum(m[i], jnp.max(s, axis=-1, keepdims=True))

    acc = [jnp.zeros((bq, LANES), F32) for _ in range(2)]
    den = [jnp.zeros((bq, 1), F32) for _ in range(2)]
    for c in range(n_kc):
        vv = vall[c * kc:(c + 1) * kc, :]
        for i in range(2):
            e = jnp.exp(s_scr[i, :, c * kc:(c + 1) * kc] - m[i])
            den[i] = den[i] + jnp.sum(e, axis=-1, keepdims=True)
            acc[i] = acc[i] + jnp.dot(e.astype(BF), vv, preferred_element_type=F32)

    o = acc[0] / den[0] - lam * (acc[1] / den[1])
    ms = jnp.mean(o * o, axis=-1, keepdims=True)
    o = o * lax.rsqrt(ms + RMS_EPS) * sg_ref[...]
    o_ref[0] = (o * (1.0 - LAM_INIT)).astype(BF)


def _attention(proj, k_ctx, v_ctx, lam_params, subln_g, *, bq):
    bx, lx, _ = proj.shape
    lk_ctx = 0 if k_ctx is None else k_ctx.shape[1]
    lk = lx + lk_ctx
    kc = 256
    head_w = 2 * DIFF_DH
    q_off = 3 * SEG // head_w
    k_off = 4 * SEG // head_w
    v_off = 5 * SEG // head_w
    in_specs = [pl.BlockSpec((1, bq, head_w), lambda b, h, i: (b, i, q_off + h)),
                pl.BlockSpec((1, lx, head_w), lambda b, h, i: (b, 0, k_off + h)),
                pl.BlockSpec((1, lx, head_w), lambda b, h, i: (b, 0, v_off + h))]
    args = [proj, proj, proj]
    if lk_ctx:
        ctx_spec = pl.BlockSpec((1, lk_ctx, head_w), lambda b, h, i: (b, 0, h))
        in_specs += [ctx_spec, ctx_spec]
        args += [k_ctx, v_ctx]
    small = pl.BlockSpec((1, DIFF_DH), lambda b, h, i: (0, 0))
    in_specs += [small] * 4 + [pl.BlockSpec((1, head_w), lambda b, h, i: (0, 0))]
    args += list(lam_params) + [subln_g]
    return pl.pallas_call(
        functools.partial(_attn_kernel, lk_self=lx, lk_ctx=lk_ctx, bq=bq, kc=kc),
        out_shape=jax.ShapeDtypeStruct((bx, lx, DIFF_HEADS * head_w), BF),
        grid=(bx, DIFF_HEADS, lx // bq),
        in_specs=in_specs,
        out_specs=pl.BlockSpec((1, bq, head_w), lambda b, h, i: (b, i, h)),
        scratch_shapes=[pltpu.VMEM((lk, head_w), BF), pltpu.VMEM((lk, head_w), BF),
                        pltpu.VMEM((2, bq, lk), F32)],
        compiler_params=_params(3),
        name="diff_attn_latent" if lk_ctx else "diff_attn_ctx",
    )(*args)


def _post_kernel(x_ref, yr_ref, od_ref, gr_ref, gd_ref, gate1_ref, sh2_ref, sc2_ref, gate2_ref,
                 n2_ref, fn_ref, wr_ref, wd_ref, wo_ref, wg_ref, wu_ref, wdn_ref, o_ref):
    br = jnp.dot(yr_ref[0], wr_ref[...], preferred_element_type=F32)
    bd = jnp.dot(od_ref[0], wd_ref[...], preferred_element_type=F32)
    merged = (gr_ref[0].astype(F32) * br + gd_ref[0].astype(F32) * bd).astype(BF)
    mix = jnp.dot(merged, wo_ref[...], preferred_element_type=F32)
    x1 = x_ref[0] + gate1_ref[0] * mix
    ms = jnp.mean(x1 * x1, axis=-1, keepdims=True)
    h2 = x1 * lax.rsqrt(ms + RMS_EPS) * n2_ref[...]
    h2 = (h2 * (1.0 + sc2_ref[0]) + sh2_ref[0]).astype(BF)
    ffn = None
    for lo, hi in FFN_SPLITS:
        g = jnp.dot(h2, wg_ref[:, lo:hi], preferred_element_type=F32)
        u = jnp.dot(h2, wu_ref[:, lo:hi], preferred_element_type=F32)
        act = (g * jax.nn.sigmoid(g) * u).astype(BF)
        part = jnp.dot(act, wdn_ref[lo:hi, :], preferred_element_type=F32)
        ffn = part if ffn is None else ffn + part
    x2 = x1 + gate2_ref[0] * ffn
    ms2 = jnp.mean(x2 * x2, axis=-1, keepdims=True)
    o_ref[0] = x2 * lax.rsqrt(ms2 + RMS_EPS) * fn_ref[...]


def _post(x, y_r, o_d, proj, mod3, norm2_g, final_g, weights, *, latent, tm):
    bx, lx, _ = x.shape
    row = (lambda b: b) if latent else (lambda b: CTX_ROW)

    def tok(width, col):
        return pl.BlockSpec((1, tm, width), lambda b, m: (b, m, col))

    def mod_spec(k):
        return pl.BlockSpec((1, 1, D_MODEL), lambda b, m: (row(b), 0, k))

    def resident(shape):
        return pl.BlockSpec(shape, lambda b, m: (0, 0), pipeline_mode=pl.Buffered(1))

    vec = pl.BlockSpec((1, D_MODEL), lambda b, m: (0, 0))
    in_specs = [tok(D_MODEL, 0), tok(D_MODEL, 0), tok(D_MODEL, 0), tok(SEG, 6), tok(SEG, 7),
                mod_spec(2), mod_spec(3), mod_spec(4), mod_spec(5), vec, vec]
    in_specs += [resident(w.shape) for w in weights]
    return pl.pallas_call(
        _post_kernel,
        out_shape=jax.ShapeDtypeStruct((bx, lx, D_MODEL), F32),
        grid=(bx, lx // tm),
        in_specs=in_specs,
        out_specs=tok(D_MODEL, 0),
        compiler_params=_params(2),
        name="post_latent" if latent else "post_ctx",
    )(x, y_r, o_d, proj, proj, mod3, mod3, mod3, mod3, norm2_g, final_g, *weights)


def kernel(x_prompt, x_sample, state_ret_fwd, state_ret_bwd, cache_diff_k, cache_diff_v, c, c_ctx,
           norm1_g, norm2_g, w_ada, b_ada, w_in, b_gate, ret_decay_fwd, ret_decay_bwd, ret_gn_g,
           w_ret_out, diff_lambda_q1, diff_lambda_k1, diff_lambda_q2, diff_lambda_k2, diff_subln_g,
           w_diff_out, w_o, w_ffn_gate, w_ffn_up, w_ffn_down, final_norm_g):
    batch, seq, _ = x_prompt.shape
    dec_batch, dec_seq, _ = x_sample.shape
    past_len = cache_diff_k.shape[2]
    layer = 0

    cvec = jnp.zeros((MOD_ROWS, D_MODEL), F32).at[:dec_batch].set(c).at[CTX_ROW].set(c_ctx)
    mod = _modulation(cvec, w_ada[layer], b_ada[layer][None, :])
    mod3 = mod.reshape(MOD_ROWS, 1, 6 * D_MODEL)

    w_in_bf = w_in[layer].astype(BF)
    post_w = tuple(w[layer].astype(BF) for w in (w_ret_out, w_diff_out, w_o, w_ffn_gate, w_ffn_up, w_ffn_down))
    n1 = norm1_g[layer][None, :]
    n2 = norm2_g[layer][None, :]
    fn = final_norm_g[None, :]
    bg = b_gate[layer][None, :]
    dec_f = ret_decay_fwd[layer][None, :]
    dec_b = ret_decay_bwd[layer][None, :]
    gn = ret_gn_g[layer][None, :]
    lam_params = tuple(p[layer][None, :] for p in (diff_lambda_q1, diff_lambda_k1, diff_lambda_q2, diff_lambda_k2))
    subln = diff_subln_g[layer][None, :]

    xp_flat = x_prompt.reshape(1, batch * seq, D_MODEL)
    proj_c, k_new, v_new = _inproj(xp_flat, mod3, n1, w_in_bf, bg, None, latent=False, tm=512)
    proj_c_b = proj_c.reshape(batch, seq, W_IN_COLS)
    yr_c, s_f, s_b = _retention(proj_c_b, dec_f, dec_b, gn, None, None, emit_state=True)
    od_c = _attention(proj_c_b, None, None, lam_params, subln, bq=seq)
    y_prompt = _post(xp_flat, yr_c.reshape(1, batch * seq, -1), od_c.reshape(1, batch * seq, -1),
                     proj_c, mod3, n2, fn, post_w, latent=False, tm=256).reshape(batch, seq, D_MODEL)

    rope = _rope_tables(dec_seq)
    proj_l = _inproj(x_sample, mod3, n1, w_in_bf, bg, rope, latent=True, tm=1024)
    yr_l = _retention(proj_l, dec_f, dec_b, gn, state_ret_fwd, state_ret_bwd, emit_state=False)
    k_ctx = cache_diff_k[:, layer].reshape(dec_batch, past_len, DIFF_HEADS * 2 * DIFF_DH)
    v_ctx = cache_diff_v[:, layer].reshape(dec_batch, past_len, DIFF_HEADS * 2 * DIFF_DH)
    od_l = _attention(proj_l, k_ctx, v_ctx, lam_params, subln, bq=256)
    y_sample = _post(x_sample, yr_l, od_l, proj_l, mod3, n2, fn, post_w, latent=True, tm=256)

    new_diff_k = k_new.reshape(batch, 1, seq, DIFF_HEADS, 2, DIFF_DH)
    new_diff_v = v_new.reshape(batch, 1, seq, DIFF_HEADS, 2 * DIFF_DH)
    return (y_prompt, y_sample, s_f, s_b, new_diff_k, new_diff_v)
```

```python
import functools
import math

import jax
import jax.numpy as jnp
from jax import lax
from jax.experimental import pallas as pl
from jax.experimental.pallas import tpu as pltpu

D_MODEL = 1024
GRID_W = 64
ROPE_BASE = 10000.0
RET_HEADS = 4
RET_DK = 128
RET_DV = 256
RET_CHUNK = 128
DIFF_HEADS = 8
DIFF_DH = 64
FFN_HIDDEN = 2816
W_IN_COLS = 8192
RMS_EPS = 1e-6
GN_EPS = 1e-5
LAM_INIT = 0.8 - 0.6 * math.exp(-0.3 * 0)

SEG = 1024
N_SEG = W_IN_COLS // SEG
LANES = 128
MOD_ROWS = 8
CTX_ROW = 4
K_R_SCALE = RET_DK ** -0.5
Q_D_SCALE = DIFF_DH ** -0.5
FFN_SPLITS = ((0, 1536), (1536, FFN_HIDDEN))
PROJ_ROWS = 256
ATT_KC = 256
ATT_VROWS = LANES + 16

BF = jnp.bfloat16
F32 = jnp.float32
VMEM_LIMIT = 56 * 1024 * 1024


def _params(n_axes):
    return pltpu.CompilerParams(dimension_semantics=("arbitrary",) * n_axes,
                                vmem_limit_bytes=VMEM_LIMIT)


def _mod_kernel(c_ref, w_ref, b_ref, o_ref):
    c = c_ref[...]
    s = c * jax.nn.sigmoid(c)
    o_ref[...] = jnp.dot(s, w_ref[...], preferred_element_type=F32,
                         precision=lax.Precision.HIGHEST) + b_ref[...]


def _modulation(cvec, w_ada, b_ada):
    n_out = w_ada.shape[1]
    tn = 1024
    return pl.pallas_call(
        _mod_kernel,
        out_shape=jax.ShapeDtypeStruct((MOD_ROWS, n_out), F32),
        grid=(n_out // tn,),
        in_specs=[pl.BlockSpec((MOD_ROWS, D_MODEL), lambda n: (0, 0)),
                  pl.BlockSpec((D_MODEL, tn), lambda n: (0, n)),
                  pl.BlockSpec((1, tn), lambda n: (0, n))],
        out_specs=pl.BlockSpec((MOD_ROWS, tn), lambda n: (0, n)),
        compiler_params=_params(1),
        name="modulation",
    )(cvec, w_ada, b_ada)


def _rope_tables(seq_len):
    rows = seq_len // GRID_W
    pos_r = jnp.repeat(jnp.arange(rows), GRID_W).astype(F32)
    pos_c = jnp.tile(jnp.arange(GRID_W), rows).astype(F32)

    def angles(d):
        n = d // 4
        inv = ROPE_BASE ** (-jnp.arange(n, dtype=F32) / n)
        return jnp.concatenate([pos_r[:, None] * inv, pos_c[:, None] * inv], axis=-1)

    ang_r = angles(RET_DK)
    cos_r = jnp.concatenate([jnp.cos(ang_r)] * 2, axis=-1)
    sin_r = jnp.concatenate([-jnp.sin(ang_r), jnp.sin(ang_r)], axis=-1)
    ang_d = angles(DIFF_DH)
    zero = jnp.zeros_like(ang_d)
    cos_d = jnp.concatenate([jnp.cos(ang_d)] * 4, axis=-1)
    sin_lo = jnp.concatenate([zero, jnp.sin(ang_d)] * 2, axis=-1)
    sin_hi = jnp.concatenate([-jnp.sin(ang_d), zero] * 2, axis=-1)
    return cos_r, sin_r, cos_d, sin_lo, sin_hi


def _inproj_kernel(*refs, latent, tm):
    if latent:
        (x_ref, sh_ref, sc_ref, g1_ref, w_ref, bg_ref,
         cr_ref, sr_ref, cd_ref, slo_ref, shi_ref, o_ref, h_scr) = refs
    else:
        x_ref, sh_ref, sc_ref, g1_ref, w_ref, bg_ref, o_ref, ko_ref, vo_ref, h_scr = refs
    n = pl.program_id(2)
    heads = SEG // LANES
    groups = [slice(r * PROJ_ROWS, (r + 1) * PROJ_ROWS) for r in range(tm // PROJ_ROWS)]

    def normed(rs):
        x = x_ref[0, rs, :]
        ms = jnp.mean(x * x, axis=-1, keepdims=True)
        y = x * lax.rsqrt(ms + RMS_EPS) * g1_ref[...]
        return (y * (1.0 + sc_ref[0]) + sh_ref[0]).astype(BF)

    def project(rs):
        return jnp.dot(h_scr[rs, :], w_ref[...], preferred_element_type=F32)

    def rope_r(a, rs):
        return a * cr_ref[rs, :] + pltpu.roll(a, 64, 1) * sr_ref[rs, :]

    def rope_d(a, rs):
        return (a * cd_ref[rs, :] + pltpu.roll(a, 32, 1) * slo_ref[rs, :]
                + pltpu.roll(a, 96, 1) * shi_ref[rs, :])

    def per_head(rs, acc, fn):
        for j in range(heads):
            cs = slice(j * LANES, (j + 1) * LANES)
            o_ref[0, rs, cs] = fn(acc[:, cs], j).astype(BF)

    @pl.when(n == 0)
    def _():
        def fn(a, j):
            return a * K_R_SCALE if j >= heads // 2 else a

        for rs in groups:
            h = normed(rs)
            h_scr[rs, :] = h
            acc = jnp.dot(h, w_ref[...], preferred_element_type=F32)
            per_head(rs, acc, (lambda a, j, rs=rs: fn(rope_r(a, rs), j)) if latent else fn)

    @pl.when(n == 1)
    def _():
        for rs in groups:
            o_ref[0, rs, :] = project(rs).astype(BF)

    @pl.when(n == 2)
    def _():
        for rs in groups:
            acc = project(rs)
            o_ref[0, rs, :] = (acc * jax.nn.sigmoid(acc)).astype(BF)

    @pl.when(n == 3)
    def _():
        for rs in groups:
            acc = project(rs)
            if latent:
                per_head(rs, acc, lambda a, j, rs=rs: rope_d(a, rs) * Q_D_SCALE)
            else:
                o_ref[0, rs, :] = (acc * Q_D_SCALE).astype(BF)

    @pl.when(n == 4)
    def _():
        for rs in groups:
            acc = project(rs)
            if latent:
                per_head(rs, acc, lambda a, j, rs=rs: rope_d(a, rs))
            else:
                o_ref[0, rs, :] = acc.astype(BF)
                ko_ref[0, rs, :] = acc

    @pl.when(n == 5)
    def _():
        for rs in groups:
            acc = project(rs)
            o_ref[0, rs, :] = acc.astype(BF)
            if not latent:
                vo_ref[0, rs, :] = acc

    @pl.when(n >= 6)
    def _():
        for rs in groups:
            o_ref[0, rs, :] = jax.nn.sigmoid(project(rs) + bg_ref[...]).astype(BF)


def _inproj(x, mod3, norm_g, w_in_bf, b_gate, rope, *, latent, tm):
    bx, lx, _ = x.shape
    row = (lambda b: b) if latent else (lambda b: CTX_ROW)

    def mod_spec(k):
        return pl.BlockSpec((1, 1, D_MODEL), lambda b, m, n: (row(b), 0, k))

    in_specs = [pl.BlockSpec((1, tm, D_MODEL), lambda b, m, n: (b, m, 0)),
                mod_spec(0), mod_spec(1),
                pl.BlockSpec((1, D_MODEL), lambda b, m, n: (0, 0)),
                pl.BlockSpec((D_MODEL, SEG), lambda b, m, n: (0, n)),
                pl.BlockSpec((1, SEG), lambda b, m, n: (0, jnp.maximum(n - 6, 0)))]
    args = [x, mod3, mod3, norm_g, w_in_bf, b_gate]
    proj_shape = jax.ShapeDtypeStruct((bx, lx, W_IN_COLS), BF)
    proj_spec = pl.BlockSpec((1, tm, SEG), lambda b, m, n: (b, m, n))
    if latent:
        in_specs += [pl.BlockSpec((tm, LANES), lambda b, m, n: (m, 0))] * 5
        args += list(rope)
        out_shape, out_specs = proj_shape, proj_spec
    else:
        kv_shape = jax.ShapeDtypeStruct((bx, lx, SEG), F32)
        kv_spec = pl.BlockSpec((1, tm, SEG), lambda b, m, n: (b, m, 0))
        out_shape, out_specs = (proj_shape, kv_shape, kv_shape), (proj_spec, kv_spec, kv_spec)
    return pl.pallas_call(
        functools.partial(_inproj_kernel, latent=latent, tm=tm),
        out_shape=out_shape,
        grid=(bx, lx // tm, N_SEG),
        in_specs=in_specs,
        out_specs=out_specs,
        scratch_shapes=[pltpu.VMEM((tm, D_MODEL), BF)],
        compiler_params=_params(3),
        name="inproj_latent" if latent else "inproj_ctx",
    )(*args)


def _ret_kernel(*refs, n_chunks, has_s0, emit_state):
    it = iter(refs)
    q_ref, k_ref, v_ref, g_ref, df_ref, db_ref, gn_ref = (next(it) for _ in range(7))
    s0f_ref, s0b_ref = (next(it), next(it)) if has_s0 else (None, None)
    y_ref = next(it)
    sfo_ref, sbo_ref = (next(it), next(it)) if emit_state else (None, None)
    sf_scr, sb_scr = next(it), next(it)
    C = RET_CHUNK
    unroll = min(4, n_chunks)
    h = pl.program_id(1)
    lane = lax.broadcasted_iota(jnp.int32, (1, RET_HEADS), 1)

    def head_log_gamma(ref):
        lg = jax.nn.log_sigmoid(ref[...].astype(F32))
        return jnp.sum(jnp.where(lane == h, lg, 0.0), axis=-1, keepdims=True)

    lgf, lgb = head_log_gamma(df_ref), head_log_gamma(db_ref)
    ri = lax.broadcasted_iota(jnp.int32, (C, C), 0).astype(F32)
    ci = lax.broadcasted_iota(jnp.int32, (C, C), 1).astype(F32)
    dist = ri - ci
    decay = jnp.where(dist >= 0, jnp.exp(lgf * jnp.maximum(dist, 0.0)), jnp.exp(lgb * jnp.maximum(-dist, 0.0)))
    rk = lax.broadcasted_iota(jnp.int32, (C, RET_DK), 0).astype(F32)
    wq_f = jnp.exp(lgf * (rk + 1.0))
    wq_b = jnp.exp(lgb * (C - rk))
    wk_f = jnp.exp(lgf * (C - 1.0 - rk))
    wk_b = jnp.exp(lgb * rk)
    gc_f = jnp.exp(lgf * C)
    gc_b = jnp.exp(lgb * C)

    def rows(c):
        return pl.ds(pl.multiple_of(c * C, C), C)

    def kv_update(c, state, wk, gc):
        kw = (k_ref[0, rows(c), :].astype(F32) * wk).astype(BF)
        kv = lax.dot_general(kw, v_ref[0, rows(c), :], (((0,), (0,)), ((), ())),
                             preferred_element_type=F32)
        return gc * state + kv

    def scan_body(n, carry):
        sf, sb = carry
        cb = n_chunks - 1 - n
        sf_scr[n] = sf.astype(BF)
        sb_scr[cb] = sb.astype(BF)
        return kv_update(n, sf, wk_f, gc_f), kv_update(cb, sb, wk_b, gc_b)

    if has_s0:
        init = (s0f_ref[...].astype(F32), s0b_ref[...].astype(F32))
    else:
        init = (jnp.zeros((RET_DK, RET_DV), F32), jnp.zeros((RET_DK, RET_DV), F32))
    sf, sb = lax.fori_loop(0, n_chunks, scan_body, init, unroll=unroll)
    if emit_state:
        sfo_ref[...] = sf
        sbo_ref[...] = sb

    gn = gn_ref[...]

    def out_body(c, _):
        q = q_ref[0, rows(c), :]
        qf32 = q.astype(F32)
        sc = lax.dot_general(q, k_ref[0, rows(c), :], (((1,), (1,)), ((), ())),
                             preferred_element_type=F32)
        o = jnp.dot((sc * decay).astype(BF), v_ref[0, rows(c), :], preferred_element_type=F32)
        o = o + jnp.dot((qf32 * wq_f).astype(BF), sf_scr[c], preferred_element_type=F32)
        o = o + jnp.dot((qf32 * wq_b).astype(BF), sb_scr[c], preferred_element_type=F32)
        mu = jnp.mean(o, axis=-1, keepdims=True)
        oc = o - mu
        var = jnp.mean(oc * oc, axis=-1, keepdims=True)
        y = oc * lax.rsqrt(var + GN_EPS) * gn
        y_ref[0, rows(c), :] = (g_ref[0, rows(c), :].astype(F32) * y).astype(BF)
        return 0

    lax.fori_loop(0, n_chunks, out_body, 0, unroll=unroll)


def _retention(proj, decay_f, decay_b, gn_g, s0_f, s0_b, *, emit_state):
    bx, lx, _ = proj.shape
    n_chunks = lx // RET_CHUNK
    has_s0 = s0_f is not None
    qk_off = (RET_HEADS * RET_DK) // RET_DK
    v_off = (2 * RET_HEADS * RET_DK) // RET_DV
    g_off = v_off + RET_HEADS
    state_spec = pl.BlockSpec((None, None, None, RET_DK, RET_DV), lambda b, h: (b, 0, h, 0, 0))
    in_specs = [pl.BlockSpec((1, lx, RET_DK), lambda b, h: (b, 0, h)),
                pl.BlockSpec((1, lx, RET_DK), lambda b, h: (b, 0, qk_off + h)),
                pl.BlockSpec((1, lx, RET_DV), lambda b, h: (b, 0, v_off + h)),
                pl.BlockSpec((1, lx, RET_DV), lambda b, h: (b, 0, g_off + h)),
                pl.BlockSpec((1, RET_HEADS), lambda b, h: (0, 0)),
                pl.BlockSpec((1, RET_HEADS), lambda b, h: (0, 0)),
                pl.BlockSpec((1, RET_DV), lambda b, h: (0, h))]
    args = [proj, proj, proj, proj, decay_f, decay_b, gn_g]
    if has_s0:
        in_specs += [state_spec, state_spec]
        args += [s0_f, s0_b]
    y_shape = jax.ShapeDtypeStruct((bx, lx, RET_HEADS * RET_DV), BF)
    y_spec = pl.BlockSpec((1, lx, RET_DV), lambda b, h: (b, 0, h))
    if emit_state:
        st_shape = jax.ShapeDtypeStruct((bx, 1, RET_HEADS, RET_DK, RET_DV), F32)
        out_shape, out_specs = (y_shape, st_shape, st_shape), (y_spec, state_spec, state_spec)
    else:
        out_shape, out_specs = y_shape, y_spec
    snap = pltpu.VMEM((n_chunks, RET_DK, RET_DV), BF)
    return pl.pallas_call(
        functools.partial(_ret_kernel, n_chunks=n_chunks, has_s0=has_s0, emit_state=emit_state),
        out_shape=out_shape,
        grid=(bx, RET_HEADS),
        in_specs=in_specs,
        out_specs=out_specs,
        scratch_shapes=[snap, snap],
        compiler_params=_params(2),
        name="retention_latent" if has_s0 else "retention_ctx",
    )(*args)


def _attn_kernel(*refs, lx, lk_ctx, bq):
    it = iter(refs)
    q_ref, k_ref, v_ref = next(it), next(it), next(it)
    kctx_ref, vctx_ref = (next(it), next(it)) if lk_ctx else (None, None)
    lq1_ref, lk1_ref, lq2_ref, lk2_ref, sg_ref = (next(it) for _ in range(5))
    o_ref = next(it)
    kall, vt, s_scr, m_scr = next(it), next(it), next(it), next(it)
    lk = lx + lk_ctx
    n_kc = lk // ATT_KC
    n_q = lx // bq

    kall[0:lx, :] = k_ref[0]
    if lk_ctx:
        kall[lx:lk, :] = kctx_ref[0].astype(BF)
    for c in range(n_kc):
        lo = c * ATT_KC
        if lo < lx:
            vc = v_ref[0, lo:lo + ATT_KC, :].astype(F32)
        else:
            vc = vctx_ref[0, lo - lx:lo - lx + ATT_KC, :]
        vt[0:LANES, lo:lo + ATT_KC] = vc.T.astype(BF)
    vt[LANES:ATT_VROWS, :] = jnp.ones((ATT_VROWS - LANES, lk), BF)

    def lam_term(a_ref, b_ref):
        return jnp.exp(jnp.sum(a_ref[...].astype(F32) * b_ref[...].astype(F32), axis=-1, keepdims=True))

    lam = lam_term(lq1_ref, lk1_ref) - lam_term(lq2_ref, lk2_ref) + LAM_INIT
    lane = lax.broadcasted_iota(jnp.int32, (bq, LANES), 1)

    def rows(t):
        return pl.ds(pl.multiple_of(t * bq, bq), bq)

    def scores(t, slot):
        q = q_ref[0, rows(t), :]
        zero = jnp.zeros_like(q)
        q_maps = (jnp.where(lane < DIFF_DH, q, zero), jnp.where(lane >= DIFF_DH, q, zero))
        m = [jnp.full((8, bq), -jnp.inf, F32) for _ in range(2)]
        for c in range(n_kc):
            kk = kall[c * ATT_KC:(c + 1) * ATT_KC, :]
            for i in range(2):
                s = lax.dot_general(kk, q_maps[i], (((1,), (1,)), ((), ())), preferred_element_type=F32)
                s_scr[slot, i, c * ATT_KC:(c + 1) * ATT_KC, :] = s
                m[i] = jnp.maximum(m[i], jnp.max(s.reshape(ATT_KC // 8, 8, bq), axis=0))
        for i in range(2):
            m_scr[slot, i] = jnp.broadcast_to(jnp.max(m[i], axis=0, keepdims=True), (8, bq))

    def values(t, slot):
        outs = []
        for i in range(2):
            m = m_scr[slot, i][0:1, :]
            acc = jnp.zeros((ATT_VROWS, bq), F32)
            for c in range(n_kc):
                e = jnp.exp(s_scr[slot, i, c * ATT_KC:(c + 1) * ATT_KC, :] - m).astype(BF)
                acc = acc + jnp.dot(vt[:, c * ATT_KC:(c + 1) * ATT_KC], e, preferred_element_type=F32)
            outs.append(acc[0:LANES, :] / acc[LANES:LANES + 1, :])
        ot = outs[0] - lam * outs[1]
        ms = jnp.mean(ot * ot, axis=0, keepdims=True)
        o = (ot * lax.rsqrt(ms + RMS_EPS)).T * sg_ref[...]
        o_ref[0, rows(t), :] = (o * (1.0 - LAM_INIT)).astype(BF)

    scores(0, 0)

    def body(t, _):
        values(t - 1, (t - 1) % 2)
        scores(t, t % 2)
        return 0

    lax.fori_loop(1, n_q, body, 0)
    values(n_q - 1, (n_q - 1) % 2)


def _attention(proj, k_ctx, v_ctx, lam_params, subln_g, *, bq):
    bx, lx, _ = proj.shape
    lk_ctx = 0 if k_ctx is None else k_ctx.shape[1]
    lk = lx + lk_ctx
    head_w = 2 * DIFF_DH
    q_off = 3 * SEG // head_w
    k_off = 4 * SEG // head_w
    v_off = 5 * SEG // head_w

    def head(off):
        return pl.BlockSpec((1, lx, head_w), lambda b, h: (b, 0, off + h))

    in_specs = [head(q_off), head(k_off), head(v_off)]
    args = [proj, proj, proj]
    if lk_ctx:
        ctx_spec = pl.BlockSpec((1, lk_ctx, head_w), lambda b, h: (b, 0, h))
        in_specs += [ctx_spec, ctx_spec]
        args += [k_ctx, v_ctx]
    small = pl.BlockSpec((1, DIFF_DH), lambda b, h: (0, 0))
    in_specs += [small] * 4 + [pl.BlockSpec((1, head_w), lambda b, h: (0, 0))]
    args += list(lam_params) + [subln_g]
    return pl.pallas_call(
        functools.partial(_attn_kernel, lx=lx, lk_ctx=lk_ctx, bq=bq),
        out_shape=jax.ShapeDtypeStruct((bx, lx, DIFF_HEADS * head_w), BF),
        grid=(bx, DIFF_HEADS),
        in_specs=in_specs,
        out_specs=head(0),
        scratch_shapes=[pltpu.VMEM((lk, head_w), BF), pltpu.VMEM((ATT_VROWS, lk), BF),
                        pltpu.VMEM((2, 2, lk, bq), F32), pltpu.VMEM((2, 2, 8, bq), F32)],
        compiler_params=_params(2),
        name="diff_attn_latent" if lk_ctx else "diff_attn_ctx",
    )(*args)


def _post_kernel(x_ref, yr_ref, od_ref, gr_ref, gd_ref, gate1_ref, sh2_ref, sc2_ref, gate2_ref,
                 n2_ref, fn_ref, wr_ref, wd_ref, wo_ref, wg_ref, wu_ref, wdn_ref, o_ref):
    br = jnp.dot(yr_ref[0], wr_ref[...], preferred_element_type=F32)
    bd = jnp.dot(od_ref[0], wd_ref[...], preferred_element_type=F32)
    merged = (gr_ref[0].astype(F32) * br + gd_ref[0].astype(F32) * bd).astype(BF)
    mix = jnp.dot(merged, wo_ref[...], preferred_element_type=F32)
    x1 = x_ref[0] + gate1_ref[0] * mix
    ms = jnp.mean(x1 * x1, axis=-1, keepdims=True)
    h2 = x1 * lax.rsqrt(ms + RMS_EPS) * n2_ref[...]
    h2 = (h2 * (1.0 + sc2_ref[0]) + sh2_ref[0]).astype(BF)
    ffn = None
    for lo, hi in FFN_SPLITS:
        g = jnp.dot(h2, wg_ref[:, lo:hi], preferred_element_type=F32)
        u = jnp.dot(h2, wu_ref[:, lo:hi], preferred_element_type=F32)
        act = (g * jax.nn.sigmoid(g) * u).astype(BF)
        part = jnp.dot(act, wdn_ref[lo:hi, :], preferred_element_type=F32)
        ffn = part if ffn is None else ffn + part
    x2 = x1 + gate2_ref[0] * ffn
    ms2 = jnp.mean(x2 * x2, axis=-1, keepdims=True)
    o_ref[0] = x2 * lax.rsqrt(ms2 + RMS_EPS) * fn_ref[...]


def _post(x, y_r, o_d, proj, mod3, norm2_g, final_g, weights, *, latent, tm):
    bx, lx, _ = x.shape
    row = (lambda b: b) if latent else (lambda b: CTX_ROW)

    def tok(width, col):
        return pl.BlockSpec((1, tm, width), lambda b, m: (b, m, col))

    def mod_spec(k):
        return pl.BlockSpec((1, 1, D_MODEL), lambda b, m: (row(b), 0, k))

    def resident(shape):
        return pl.BlockSpec(shape, lambda b, m: (0, 0), pipeline_mode=pl.Buffered(1))

    vec = pl.BlockSpec((1, D_MODEL), lambda b, m: (0, 0))
    in_specs = [tok(D_MODEL, 0), tok(D_MODEL, 0), tok(D_MODEL, 0), tok(SEG, 6), tok(SEG, 7),
                mod_spec(2), mod_spec(3), mod_spec(4), mod_spec(5), vec, vec]
    in_specs += [resident(w.shape) for w in weights]
    return pl.pallas_call(
        _post_kernel,
        out_shape=jax.ShapeDtypeStruct((bx, lx, D_MODEL), F32),
        grid=(bx, lx // tm),
        in_specs=in_specs,
        out_specs=tok(D_MODEL, 0),
        compiler_params=_params(2),
        name="post_latent" if latent else "post_ctx",
    )(x, y_r, o_d, proj, proj, mod3, mod3, mod3, mod3, norm2_g, final_g, *weights)


def kernel(x_prompt, x_sample, state_ret_fwd, state_ret_bwd, cache_diff_k, cache_diff_v, c, c_ctx,
           norm1_g, norm2_g, w_ada, b_ada, w_in, b_gate, ret_decay_fwd, ret_decay_bwd, ret_gn_g,
           w_ret_out, diff_lambda_q1, diff_lambda_k1, diff_lambda_q2, diff_lambda_k2, diff_subln_g,
           w_diff_out, w_o, w_ffn_gate, w_ffn_up, w_ffn_down, final_norm_g):
    batch, seq, _ = x_prompt.shape
    dec_batch, dec_seq, _ = x_sample.shape
    past_len = cache_diff_k.shape[2]
    layer = 0

    cvec = jnp.zeros((MOD_ROWS, D_MODEL), F32).at[:dec_batch].set(c).at[CTX_ROW].set(c_ctx)
    mod = _modulation(cvec, w_ada[layer], b_ada[layer][None, :])
    mod3 = mod.reshape(MOD_ROWS, 1, 6 * D_MODEL)

    w_in_bf = w_in[layer].astype(BF)
    post_w = tuple(w[layer].astype(BF) for w in (w_ret_out, w_diff_out, w_o, w_ffn_gate, w_ffn_up, w_ffn_down))
    n1 = norm1_g[layer][None, :]
    n2 = norm2_g[layer][None, :]
    fn = final_norm_g[None, :]
    bg = b_gate[layer][None, :]
    dec_f = ret_decay_fwd[layer][None, :]
    dec_b = ret_decay_bwd[layer][None, :]
    gn = ret_gn_g[layer][None, :]
    lam_params = tuple(p[layer][None, :] for p in (diff_lambda_q1, diff_lambda_k1, diff_lambda_q2, diff_lambda_k2))
    subln = diff_subln_g[layer][None, :]

    xp_flat = x_prompt.reshape(1, batch * seq, D_MODEL)
    proj_c, k_new, v_new = _inproj(xp_flat, mod3, n1, w_in_bf, bg, None, latent=False, tm=512)
    proj_c_b = proj_c.reshape(batch, seq, W_IN_COLS)
    yr_c, s_f, s_b = _retention(proj_c_b, dec_f, dec_b, gn, None, None, emit_state=True)
    od_c = _attention(proj_c_b, None, None, lam_params, subln, bq=seq)
    y_prompt = _post(xp_flat, yr_c.reshape(1, batch * seq, -1), od_c.reshape(1, batch * seq, -1),
                     proj_c, mod3, n2, fn, post_w, latent=False, tm=256).reshape(batch, seq, D_MODEL)

    rope = _rope_tables(dec_seq)
    proj_l = _inproj(x_sample, mod3, n1, w_in_bf, bg, rope, latent=True, tm=1024)
    yr_l = _retention(proj_l, dec_f, dec_b, gn, state_ret_fwd, state_ret_bwd, emit_state=False)
    k_ctx = cache_diff_k[:, layer].reshape(dec_batch, past_len, DIFF_HEADS * 2 * DIFF_DH)
    v_ctx = cache_diff_v[:, layer].reshape(dec_batch, past_len, DIFF_HEADS * 2 * DIFF_DH)
    od_l = _attention(proj_l, k_ctx, v_ctx, lam_params, subln, bq=256)
    y_sample = _post(x_sample, yr_l, od_l, proj_l, mod3, n2, fn, post_w, latent=True, tm=256)

    new_diff_k = k_new.reshape(batch, 1, seq, DIFF_HEADS, 2, DIFF_DH)
    new_diff_v = v_new.reshape(batch, 1, seq, DIFF_HEADS, 2 * DIFF_DH)
    return (y_prompt, y_sample, s_f, s_b, new_diff_k, new_diff_v)
```

```python
import functools
import math

import jax
import jax.numpy as jnp
from jax import lax
from jax.experimental import pallas as pl
from jax.experimental.pallas import tpu as pltpu

D_MODEL = 1024
GRID_W = 64
ROPE_BASE = 10000.0
RET_HEADS = 4
RET_DK = 128
RET_DV = 256
RET_CHUNK = 128
DIFF_HEADS = 8
DIFF_DH = 64
FFN_HIDDEN = 2816
W_IN_COLS = 8192
RMS_EPS = 1e-6
GN_EPS = 1e-5
LAM_INIT = 0.8 - 0.6 * math.exp(-0.3 * 0)

SEG = 1024
N_SEG = W_IN_COLS // SEG
LANES = 128
MOD_ROWS = 8
CTX_ROW = 4
K_R_SCALE = RET_DK ** -0.5
Q_D_SCALE = DIFF_DH ** -0.5
FFN_SPLITS = ((0, 1536), (1536, FFN_HIDDEN))
PROJ_ROWS = 256
ATT_KC = 256
ATT_VROWS = LANES + 16

BF = jnp.bfloat16
F32 = jnp.float32
VMEM_LIMIT = 56 * 1024 * 1024


def _params(n_axes):
    return pltpu.CompilerParams(dimension_semantics=("arbitrary",) * n_axes,
                                vmem_limit_bytes=VMEM_LIMIT)


def _mod_kernel(c_ref, w_ref, b_ref, o_ref):
    c = c_ref[...]
    s = c * jax.nn.sigmoid(c)
    o_ref[...] = jnp.dot(s, w_ref[...], preferred_element_type=F32,
                         precision=lax.Precision.HIGHEST) + b_ref[...]


def _modulation(cvec, w_ada, b_ada):
    n_out = w_ada.shape[1]
    tn = 1024
    return pl.pallas_call(
        _mod_kernel,
        out_shape=jax.ShapeDtypeStruct((MOD_ROWS, n_out), F32),
        grid=(n_out // tn,),
        in_specs=[pl.BlockSpec((MOD_ROWS, D_MODEL), lambda n: (0, 0)),
                  pl.BlockSpec((D_MODEL, tn), lambda n: (0, n)),
                  pl.BlockSpec((1, tn), lambda n: (0, n))],
        out_specs=pl.BlockSpec((MOD_ROWS, tn), lambda n: (0, n)),
        compiler_params=_params(1),
        name="modulation",
    )(cvec, w_ada, b_ada)


def _rope_tables(seq_len):
    rows = seq_len // GRID_W
    pos_r = jnp.repeat(jnp.arange(rows), GRID_W).astype(F32)
    pos_c = jnp.tile(jnp.arange(GRID_W), rows).astype(F32)

    def angles(d):
        n = d // 4
        inv = ROPE_BASE ** (-jnp.arange(n, dtype=F32) / n)
        return jnp.concatenate([pos_r[:, None] * inv, pos_c[:, None] * inv], axis=-1)

    ang_r = angles(RET_DK)
    cos_r = jnp.concatenate([jnp.cos(ang_r)] * 2, axis=-1)
    sin_r = jnp.concatenate([-jnp.sin(ang_r), jnp.sin(ang_r)], axis=-1)
    ang_d = angles(DIFF_DH)
    zero = jnp.zeros_like(ang_d)
    cos_d = jnp.concatenate([jnp.cos(ang_d)] * 4, axis=-1)
    sin_lo = jnp.concatenate([zero, jnp.sin(ang_d)] * 2, axis=-1)
    sin_hi = jnp.concatenate([-jnp.sin(ang_d), zero] * 2, axis=-1)
    return cos_r, sin_r, cos_d, sin_lo, sin_hi


def _inproj_kernel(*refs, latent, tm):
    if latent:
        (x_ref, sh_ref, sc_ref, g1_ref, w_ref, bg_ref,
         cr_ref, sr_ref, cd_ref, slo_ref, shi_ref, o_ref, h_scr) = refs
    else:
        x_ref, sh_ref, sc_ref, g1_ref, w_ref, bg_ref, o_ref, ko_ref, vo_ref, h_scr = refs
    n = pl.program_id(2)
    heads = SEG // LANES
    groups = [slice(r * PROJ_ROWS, (r + 1) * PROJ_ROWS) for r in range(tm // PROJ_ROWS)]

    def normed(rs):
        x = x_ref[0, rs, :]
        ms = jnp.mean(x * x, axis=-1, keepdims=True)
        y = x * lax.rsqrt(ms + RMS_EPS) * g1_ref[...]
        return (y * (1.0 + sc_ref[0]) + sh_ref[0]).astype(BF)

    def project(rs):
        return jnp.dot(h_scr[rs, :], w_ref[...], preferred_element_type=F32)

    def rope_r(a, rs):
        return a * cr_ref[rs, :] + pltpu.roll(a, 64, 1) * sr_ref[rs, :]

    def rope_d(a, rs):
        return (a * cd_ref[rs, :] + pltpu.roll(a, 32, 1) * slo_ref[rs, :]
                + pltpu.roll(a, 96, 1) * shi_ref[rs, :])

    def per_head(rs, acc, fn):
        for j in range(heads):
            cs = slice(j * LANES, (j + 1) * LANES)
            o_ref[0, rs, cs] = fn(acc[:, cs], j).astype(BF)

    @pl.when(n == 0)
    def _():
        def fn(a, j):
            return a * K_R_SCALE if j >= heads // 2 else a

        for rs in groups:
            h = normed(rs)
            h_scr[rs, :] = h
            acc = jnp.dot(h, w_ref[...], preferred_element_type=F32)
            per_head(rs, acc, (lambda a, j, rs=rs: fn(rope_r(a, rs), j)) if latent else fn)

    @pl.when(n == 1)
    def _():
        for rs in groups:
            o_ref[0, rs, :] = project(rs).astype(BF)

    @pl.when(n == 2)
    def _():
        for rs in groups:
            acc = project(rs)
            o_ref[0, rs, :] = (acc * jax.nn.sigmoid(acc)).astype(BF)

    @pl.when(n == 3)
    def _():
        for rs in groups:
            acc = project(rs)
            if latent:
                per_head(rs, acc, lambda a, j, rs=rs: rope_d(a, rs) * Q_D_SCALE)
            else:
                o_ref[0, rs, :] = (acc * Q_D_SCALE).astype(BF)

    @pl.when(n == 4)
    def _():
        for rs in groups:
            acc = project(rs)
            if latent:
                per_head(rs, acc, lambda a, j, rs=rs: rope_d(a, rs))
            else:
                o_ref[0, rs, :] = acc.astype(BF)
                ko_ref[0, rs, :] = acc

    @pl.when(n == 5)
    def _():
        for rs in groups:
            acc = project(rs)
            o_ref[0, rs, :] = acc.astype(BF)
            if not latent:
                vo_ref[0, rs, :] = acc

    @pl.when(n >= 6)
    def _():
        for rs in groups:
            o_ref[0, rs, :] = jax.nn.sigmoid(project(rs) + bg_ref[...]).astype(BF)


def _inproj(x, mod3, norm_g, w_in_bf, b_gate, rope, *, latent, tm):
    bx, lx, _ = x.shape
    row = (lambda b: b) if latent else (lambda b: CTX_ROW)

    def mod_spec(k):
        return pl.BlockSpec((1, 1, D_MODEL), lambda b, m, n: (row(b), 0, k))

    in_specs = [pl.BlockSpec((1, tm, D_MODEL), lambda b, m, n: (b, m, 0)),
                mod_spec(0), mod_spec(1),
                pl.BlockSpec((1, D_MODEL), lambda b, m, n: (0, 0)),
                pl.BlockSpec((D_MODEL, SEG), lambda b, m, n: (0, n)),
                pl.BlockSpec((1, SEG), lambda b, m, n: (0, jnp.maximum(n - 6, 0)))]
    args = [x, mod3, mod3, norm_g, w_in_bf, b_gate]
    proj_shape = jax.ShapeDtypeStruct((bx, lx, W_IN_COLS), BF)
    proj_spec = pl.BlockSpec((1, tm, SEG), lambda b, m, n: (b, m, n))
    if latent:
        in_specs += [pl.BlockSpec((tm, LANES), lambda b, m, n: (m, 0))] * 5
        args += list(rope)
        out_shape, out_specs = proj_shape, proj_spec
    else:
        kv_shape = jax.ShapeDtypeStruct((bx, lx, SEG), F32)
        kv_spec = pl.BlockSpec((1, tm, SEG), lambda b, m, n: (b, m, 0))
        out_shape, out_specs = (proj_shape, kv_shape, kv_shape), (proj_spec, kv_spec, kv_spec)
    return pl.pallas_call(
        functools.partial(_inproj_kernel, latent=latent, tm=tm),
        out_shape=out_shape,
        grid=(bx, lx // tm, N_SEG),
        in_specs=in_specs,
        out_specs=out_specs,
        scratch_shapes=[pltpu.VMEM((tm, D_MODEL), BF)],
        compiler_params=_params(3),
        name="inproj_latent" if latent else "inproj_ctx",
    )(*args)


def _ret_kernel(*refs, n_chunks, hb, has_s0, emit_state):
    it = iter(refs)
    q_ref, k_ref, v_ref, g_ref, df_ref, db_ref, gn_ref = (next(it) for _ in range(7))
    s0f_ref, s0b_ref = (next(it), next(it)) if has_s0 else (None, None)
    y_ref = next(it)
    sfo_ref, sbo_ref = (next(it), next(it)) if emit_state else (None, None)
    sf_scr, sb_scr = next(it), next(it)
    C = RET_CHUNK
    unroll = max(1, min(8 // hb, n_chunks))
    lane = lax.broadcasted_iota(jnp.int32, (1, RET_HEADS), 1)
    lg_f_all = jax.nn.log_sigmoid(df_ref[...].astype(F32))
    lg_b_all = jax.nn.log_sigmoid(db_ref[...].astype(F32))
    ri = lax.broadcasted_iota(jnp.int32, (C, C), 0).astype(F32)
    ci = lax.broadcasted_iota(jnp.int32, (C, C), 1).astype(F32)
    dist = ri - ci
    rk = lax.broadcasted_iota(jnp.int32, (C, RET_DK), 0).astype(F32)

    def head_consts(j):
        head = pl.program_id(1) * hb + j
        lgf = jnp.sum(jnp.where(lane == head, lg_f_all, 0.0), axis=-1, keepdims=True)
        lgb = jnp.sum(jnp.where(lane == head, lg_b_all, 0.0), axis=-1, keepdims=True)
        decay = jnp.where(dist >= 0, jnp.exp(lgf * jnp.maximum(dist, 0.0)),
                          jnp.exp(lgb * jnp.maximum(-dist, 0.0)))
        return dict(decay=decay,
                    wq_f=jnp.exp(lgf * (rk + 1.0)), wq_b=jnp.exp(lgb * (C - rk)),
                    wk_f=jnp.exp(lgf * (C - 1.0 - rk)), wk_b=jnp.exp(lgb * rk),
                    gc_f=jnp.exp(lgf * C), gc_b=jnp.exp(lgb * C))

    consts = [head_consts(j) for j in range(hb)]

    def rows(c):
        return pl.ds(pl.multiple_of(c * C, C), C)

    def cols(j, width):
        return slice(j * width, (j + 1) * width)

    def kv_update(j, c, state, wk, gc):
        kw = (k_ref[0, rows(c), cols(j, RET_DK)].astype(F32) * wk).astype(BF)
        kv = lax.dot_general(kw, v_ref[0, rows(c), cols(j, RET_DV)], (((0,), (0,)), ((), ())),
                             preferred_element_type=F32)
        return gc * state + kv

    def scan_body(n, carry):
        cb = n_chunks - 1 - n
        new = []
        for j in range(hb):
            sf, sb = carry[2 * j], carry[2 * j + 1]
            sf_scr[j, n] = sf.astype(BF)
            sb_scr[j, cb] = sb.astype(BF)
            new.append(kv_update(j, n, sf, consts[j]["wk_f"], consts[j]["gc_f"]))
            new.append(kv_update(j, cb, sb, consts[j]["wk_b"], consts[j]["gc_b"]))
        return tuple(new)

    init = []
    for j in range(hb):
        if has_s0:
            init += [s0f_ref[j].astype(F32), s0b_ref[j].astype(F32)]
        else:
            init += [jnp.zeros((RET_DK, RET_DV), F32)] * 2
    final = lax.fori_loop(0, n_chunks, scan_body, tuple(init), unroll=unroll)
    if emit_state:
        for j in range(hb):
            sfo_ref[j] = final[2 * j]
            sbo_ref[j] = final[2 * j + 1]

    def out_body(c, _):
        for j in range(hb):
            k = consts[j]
            q = q_ref[0, rows(c), cols(j, RET_DK)]
            qf32 = q.astype(F32)
            v = v_ref[0, rows(c), cols(j, RET_DV)]
            sc = lax.dot_general(q, k_ref[0, rows(c), cols(j, RET_DK)], (((1,), (1,)), ((), ())),
                                 preferred_element_type=F32)
            o = jnp.dot((sc * k["decay"]).astype(BF), v, preferred_element_type=F32)
            o = o + jnp.dot((qf32 * k["wq_f"]).astype(BF), sf_scr[j, c], preferred_element_type=F32)
            o = o + jnp.dot((qf32 * k["wq_b"]).astype(BF), sb_scr[j, c], preferred_element_type=F32)
            mu = jnp.mean(o, axis=-1, keepdims=True)
            oc = o - mu
            var = jnp.mean(oc * oc, axis=-1, keepdims=True)
            y = oc * lax.rsqrt(var + GN_EPS) * gn_ref[:, cols(j, RET_DV)]
            gate = g_ref[0, rows(c), cols(j, RET_DV)].astype(F32)
            y_ref[0, rows(c), cols(j, RET_DV)] = (gate * y).astype(BF)
        return 0

    lax.fori_loop(0, n_chunks, out_body, 0, unroll=unroll)


def _retention(proj, decay_f, decay_b, gn_g, s0_f, s0_b, *, emit_state, hb):
    bx, lx, _ = proj.shape
    n_chunks = lx // RET_CHUNK
    has_s0 = s0_f is not None
    groups = RET_HEADS // hb
    k_off = (RET_HEADS * RET_DK) // (hb * RET_DK)
    v_off = (2 * RET_HEADS * RET_DK) // (hb * RET_DV)
    g_off = v_off + groups
    state_spec = pl.BlockSpec((None, None, hb, RET_DK, RET_DV), lambda b, h: (b, 0, h, 0, 0))
    in_specs = [pl.BlockSpec((1, lx, hb * RET_DK), lambda b, h: (b, 0, h)),
                pl.BlockSpec((1, lx, hb * RET_DK), lambda b, h: (b, 0, k_off + h)),
                pl.BlockSpec((1, lx, hb * RET_DV), lambda b, h: (b, 0, v_off + h)),
                pl.BlockSpec((1, lx, hb * RET_DV), lambda b, h: (b, 0, g_off + h)),
                pl.BlockSpec((1, RET_HEADS), lambda b, h: (0, 0)),
                pl.BlockSpec((1, RET_HEADS), lambda b, h: (0, 0)),
                pl.BlockSpec((1, hb * RET_DV), lambda b, h: (0, h))]
    args = [proj, proj, proj, proj, decay_f, decay_b, gn_g]
    if has_s0:
        in_specs += [state_spec, state_spec]
        args += [s0_f, s0_b]
    y_shape = jax.ShapeDtypeStruct((bx, lx, RET_HEADS * RET_DV), BF)
    y_spec = pl.BlockSpec((1, lx, hb * RET_DV), lambda b, h: (b, 0, h))
    if emit_state:
        st_shape = jax.ShapeDtypeStruct((bx, 1, RET_HEADS, RET_DK, RET_DV), F32)
        out_shape, out_specs = (y_shape, st_shape, st_shape), (y_spec, state_spec, state_spec)
    else:
        out_shape, out_specs = y_shape, y_spec
    snap = pltpu.VMEM((hb, n_chunks, RET_DK, RET_DV), BF)
    return pl.pallas_call(
        functools.partial(_ret_kernel, n_chunks=n_chunks, hb=hb, has_s0=has_s0, emit_state=emit_state),
        out_shape=out_shape,
        grid=(bx, groups),
        in_specs=in_specs,
        out_specs=out_specs,
        scratch_shapes=[snap, snap],
        compiler_params=_params(2),
        name="retention_latent" if has_s0 else "retention_ctx",
    )(*args)


def _attn_kernel(*refs, lx, lk_ctx, bq, hb):
    it = iter(refs)
    q_ref, k_ref, v_ref = next(it), next(it), next(it)
    kctx_ref, vctx_ref = (next(it), next(it)) if lk_ctx else (None, None)
    lq1_ref, lk1_ref, lq2_ref, lk2_ref, sg_ref = (next(it) for _ in range(5))
    o_ref = next(it)
    vt, s_scr, m_scr = next(it), next(it), next(it)
    lk = lx + lk_ctx
    n_kc = lk // ATT_KC
    n_q = lx // bq

    def cols(j):
        return slice(j * LANES, (j + 1) * LANES)

    def chunk(c):
        return slice(c * ATT_KC, (c + 1) * ATT_KC)

    def keys(j, c):
        lo = c * ATT_KC
        if lo < lx:
            return k_ref[0, lo:lo + ATT_KC, cols(j)]
        return kctx_ref[0, lo - lx:lo - lx + ATT_KC, cols(j)].astype(BF)

    for j in range(hb):
        for c in range(n_kc):
            lo = c * ATT_KC
            if lo < lx:
                vc = v_ref[0, lo:lo + ATT_KC, cols(j)].astype(F32)
            else:
                vc = vctx_ref[0, lo - lx:lo - lx + ATT_KC, cols(j)]
            vt[j, 0:LANES, chunk(c)] = vc.T.astype(BF)
        vt[j, LANES:ATT_VROWS, :] = jnp.ones((ATT_VROWS - LANES, lk), BF)

    def lam_term(a_ref, b_ref):
        return jnp.exp(jnp.sum(a_ref[...].astype(F32) * b_ref[...].astype(F32), axis=-1, keepdims=True))

    lam = lam_term(lq1_ref, lk1_ref) - lam_term(lq2_ref, lk2_ref) + LAM_INIT
    lane = lax.broadcasted_iota(jnp.int32, (bq, LANES), 1)

    def rows(t):
        return pl.ds(pl.multiple_of(t * bq, bq), bq)

    def scores(j, t, slot):
        q = q_ref[0, rows(t), cols(j)]
        zero = jnp.zeros_like(q)
        q_maps = (jnp.where(lane < DIFF_DH, q, zero), jnp.where(lane >= DIFF_DH, q, zero))
        m = [jnp.full((8, bq), -jnp.inf, F32) for _ in range(2)]
        for c in range(n_kc):
            kk = keys(j, c)
            for i in range(2):
                s = lax.dot_general(kk, q_maps[i], (((1,), (1,)), ((), ())),
                                    preferred_element_type=F32)
                s_scr[j, slot, i, chunk(c), :] = s
                m[i] = jnp.maximum(m[i], jnp.max(s.reshape(ATT_KC // 8, 8, bq), axis=0))
            yield
        for i in range(2):
            m_scr[j, slot, i] = jnp.broadcast_to(jnp.max(m[i], axis=0, keepdims=True), (8, bq))

    def values(j, t, slot):
        m = [m_scr[j, slot, i][0:1, :] for i in range(2)]
        acc = [jnp.zeros((ATT_VROWS, bq), F32) for _ in range(2)]
        for c in range(n_kc):
            for i in range(2):
                e = jnp.exp(s_scr[j, slot, i, chunk(c), :] - m[i]).astype(BF)
                acc[i] = acc[i] + jnp.dot(vt[j, :, chunk(c)], e, preferred_element_type=F32)
            yield
        outs = [a[0:LANES, :] / a[LANES:LANES + 1, :] for a in acc]
        ot = outs[0] - lam * outs[1]
        ms = jnp.mean(ot * ot, axis=0, keepdims=True)
        o = (ot * lax.rsqrt(ms + RMS_EPS)).T * sg_ref[...]
        o_ref[0, rows(t), cols(j)] = (o * (1.0 - LAM_INIT)).astype(BF)

    def chain(*streams):
        for g in streams:
            yield from g

    def run(*streams):
        live = list(streams)
        while live:
            live = [g for g in live if next(g, StopIteration) is not StopIteration]

    if n_q == 1:
        run(*[chain(scores(j, 0, 0), values(j, 0, 0)) for j in range(hb)])
        return
    assert hb == 1 and n_q % 2 == 0
    run(scores(0, 0, 0))

    def body(u, _):
        t = 2 * u
        run(scores(0, t + 1, 1), values(0, t, 0))
        run(scores(0, t + 2, 0), values(0, t + 1, 1))
        return 0

    lax.fori_loop(0, (n_q - 2) // 2, body, 0)
    run(scores(0, n_q - 1, 1), values(0, n_q - 2, 0))
    run(values(0, n_q - 1, 1))


def _attention(proj, k_ctx, v_ctx, lam_params, subln_g, *, bq, hb):
    bx, lx, _ = proj.shape
    lk_ctx = 0 if k_ctx is None else k_ctx.shape[1]
    lk = lx + lk_ctx
    head_w = 2 * DIFF_DH
    width = hb * head_w
    q_off = 3 * SEG // width
    k_off = 4 * SEG // width
    v_off = 5 * SEG // width

    def heads(off):
        return pl.BlockSpec((1, lx, width), lambda b, h: (b, 0, off + h))

    in_specs = [heads(q_off), heads(k_off), heads(v_off)]
    args = [proj, proj, proj]
    if lk_ctx:
        ctx_spec = pl.BlockSpec((1, lk_ctx, width), lambda b, h: (b, 0, h))
        in_specs += [ctx_spec, ctx_spec]
        args += [k_ctx, v_ctx]
    small = pl.BlockSpec((1, DIFF_DH), lambda b, h: (0, 0))
    in_specs += [small] * 4 + [pl.BlockSpec((1, head_w), lambda b, h: (0, 0))]
    args += list(lam_params) + [subln_g]
    return pl.pallas_call(
        functools.partial(_attn_kernel, lx=lx, lk_ctx=lk_ctx, bq=bq, hb=hb),
        out_shape=jax.ShapeDtypeStruct((bx, lx, DIFF_HEADS * head_w), BF),
        grid=(bx, DIFF_HEADS // hb),
        in_specs=in_specs,
        out_specs=heads(0),
        scratch_shapes=[pltpu.VMEM((hb, ATT_VROWS, lk), BF),
                        pltpu.VMEM((hb, 2, 2, lk, bq), F32), pltpu.VMEM((hb, 2, 2, 8, bq), F32)],
        compiler_params=_params(2),
        name="diff_attn_latent" if lk_ctx else "diff_attn_ctx",
    )(*args)


def _post_kernel(x_ref, yr_ref, od_ref, gr_ref, gd_ref, gate1_ref, sh2_ref, sc2_ref, gate2_ref,
                 n2_ref, fn_ref, wr_ref, wd_ref, wo_ref, wg_ref, wu_ref, wdn_ref, o_ref):
    br = jnp.dot(yr_ref[0], wr_ref[...], preferred_element_type=F32)
    bd = jnp.dot(od_ref[0], wd_ref[...], preferred_element_type=F32)
    merged = (gr_ref[0].astype(F32) * br + gd_ref[0].astype(F32) * bd).astype(BF)
    mix = jnp.dot(merged, wo_ref[...], preferred_element_type=F32)
    x1 = x_ref[0] + gate1_ref[0] * mix
    ms = jnp.mean(x1 * x1, axis=-1, keepdims=True)
    h2 = x1 * lax.rsqrt(ms + RMS_EPS) * n2_ref[...]
    h2 = (h2 * (1.0 + sc2_ref[0]) + sh2_ref[0]).astype(BF)
    ffn = None
    for lo, hi in FFN_SPLITS:
        g = jnp.dot(h2, wg_ref[:, lo:hi], preferred_element_type=F32)
        u = jnp.dot(h2, wu_ref[:, lo:hi], preferred_element_type=F32)
        act = (g * jax.nn.sigmoid(g) * u).astype(BF)
        part = jnp.dot(act, wdn_ref[lo:hi, :], preferred_element_type=F32)
        ffn = part if ffn is None else ffn + part
    x2 = x1 + gate2_ref[0] * ffn
    ms2 = jnp.mean(x2 * x2, axis=-1, keepdims=True)
    o_ref[0] = x2 * lax.rsqrt(ms2 + RMS_EPS) * fn_ref[...]


def _post(x, y_r, o_d, proj, mod3, norm2_g, final_g, weights, *, latent, tm):
    bx, lx, _ = x.shape
    row = (lambda b: b) if latent else (lambda b: CTX_ROW)

    def tok(width, col):
        return pl.BlockSpec((1, tm, width), lambda b, m: (b, m, col))

    def mod_spec(k):
        return pl.BlockSpec((1, 1, D_MODEL), lambda b, m: (row(b), 0, k))

    def resident(shape):
        return pl.BlockSpec(shape, lambda b, m: (0, 0), pipeline_mode=pl.Buffered(1))

    vec = pl.BlockSpec((1, D_MODEL), lambda b, m: (0, 0))
    in_specs = [tok(D_MODEL, 0), tok(D_MODEL, 0), tok(D_MODEL, 0), tok(SEG, 6), tok(SEG, 7),
                mod_spec(2), mod_spec(3), mod_spec(4), mod_spec(5), vec, vec]
    in_specs += [resident(w.shape) for w in weights]
    return pl.pallas_call(
        _post_kernel,
        out_shape=jax.ShapeDtypeStruct((bx, lx, D_MODEL), F32),
        grid=(bx, lx // tm),
        in_specs=in_specs,
        out_specs=tok(D_MODEL, 0),
        compiler_params=_params(2),
        name="post_latent" if latent else "post_ctx",
    )(x, y_r, o_d, proj, proj, mod3, mod3, mod3, mod3, norm2_g, final_g, *weights)


def kernel(x_prompt, x_sample, state_ret_fwd, state_ret_bwd, cache_diff_k, cache_diff_v, c, c_ctx,
           norm1_g, norm2_g, w_ada, b_ada, w_in, b_gate, ret_decay_fwd, ret_decay_bwd, ret_gn_g,
           w_ret_out, diff_lambda_q1, diff_lambda_k1, diff_lambda_q2, diff_lambda_k2, diff_subln_g,
           w_diff_out, w_o, w_ffn_gate, w_ffn_up, w_ffn_down, final_norm_g):
    batch, seq, _ = x_prompt.shape
    dec_batch, dec_seq, _ = x_sample.shape
    past_len = cache_diff_k.shape[2]
    layer = 0

    cvec = jnp.zeros((MOD_ROWS, D_MODEL), F32).at[:dec_batch].set(c).at[CTX_ROW].set(c_ctx)
    mod = _modulation(cvec, w_ada[layer], b_ada[layer][None, :])
    mod3 = mod.reshape(MOD_ROWS, 1, 6 * D_MODEL)

    w_in_bf = w_in[layer].astype(BF)
    post_w = tuple(w[layer].astype(BF) for w in (w_ret_out, w_diff_out, w_o, w_ffn_gate, w_ffn_up, w_ffn_down))
    n1 = norm1_g[layer][None, :]
    n2 = norm2_g[layer][None, :]
    fn = final_norm_g[None, :]
    bg = b_gate[layer][None, :]
    dec_f = ret_decay_fwd[layer][None, :]
    dec_b = ret_decay_bwd[layer][None, :]
    gn = ret_gn_g[layer][None, :]
    lam_params = tuple(p[layer][None, :] for p in (diff_lambda_q1, diff_lambda_k1, diff_lambda_q2, diff_lambda_k2))
    subln = diff_subln_g[layer][None, :]

    xp_flat = x_prompt.reshape(1, batch * seq, D_MODEL)
    proj_c, k_new, v_new = _inproj(xp_flat, mod3, n1, w_in_bf, bg, None, latent=False, tm=512)
    proj_c_b = proj_c.reshape(batch, seq, W_IN_COLS)
    yr_c, s_f, s_b = _retention(proj_c_b, dec_f, dec_b, gn, None, None, emit_state=True, hb=RET_HEADS)
    od_c = _attention(proj_c_b, None, None, lam_params, subln, bq=seq, hb=DIFF_HEADS)
    y_prompt = _post(xp_flat, yr_c.reshape(1, batch * seq, -1), od_c.reshape(1, batch * seq, -1),
                     proj_c, mod3, n2, fn, post_w, latent=False, tm=256).reshape(batch, seq, D_MODEL)

    rope = _rope_tables(dec_seq)
    proj_l = _inproj(x_sample, mod3, n1, w_in_bf, bg, rope, latent=True, tm=1024)
    yr_l = _retention(proj_l, dec_f, dec_b, gn, state_ret_fwd, state_ret_bwd, emit_state=False, hb=2)
    k_ctx = cache_diff_k[:, layer].reshape(dec_batch, past_len, DIFF_HEADS * 2 * DIFF_DH)
    v_ctx = cache_diff_v[:, layer].reshape(dec_batch, past_len, DIFF_HEADS * 2 * DIFF_DH)
    od_l = _attention(proj_l, k_ctx, v_ctx, lam_params, subln, bq=256, hb=1)
    y_sample = _post(x_sample, yr_l, od_l, proj_l, mod3, n2, fn, post_w, latent=True, tm=256)

    new_diff_k = k_new.reshape(batch, 1, seq, DIFF_HEADS, 2, DIFF_DH)
    new_diff_v = v_new.reshape(batch, 1, seq, DIFF_HEADS, 2 * DIFF_DH)
    return (y_prompt, y_sample, s_f, s_b, new_diff_k, new_diff_v)
```

```python
import functools
import math

import jax
import jax.numpy as jnp
import numpy as np
from jax import lax
from jax.experimental import pallas as pl
from jax.experimental.pallas import tpu as pltpu

D_MODEL = 1024
GRID_W = 64
ROPE_BASE = 10000.0
RET_HEADS = 4
RET_DK = 128
RET_DV = 256
RET_CHUNK = 128
DIFF_HEADS = 8
DIFF_DH = 64
FFN_HIDDEN = 2816
W_IN_COLS = 8192
RMS_EPS = 1e-6
GN_EPS = 1e-5
LAM_INIT = 0.8 - 0.6 * math.exp(-0.3 * 0)

SEG = 1024
N_SEG = W_IN_COLS // SEG
LANES = 128
MOD_ROWS = 8
CTX_ROW = 4
K_R_SCALE = RET_DK ** -0.5
Q_D_SCALE = DIFF_DH ** -0.5
FFN_SPLITS = ((0, 1536), (1536, FFN_HIDDEN))
PROJ_ROWS = 256
ATT_KC = 256
ATT_VROWS = LANES + 16

BF = jnp.bfloat16
F32 = jnp.float32
VMEM_LIMIT = 56 * 1024 * 1024


def _params(n_axes):
    return pltpu.CompilerParams(dimension_semantics=("arbitrary",) * n_axes,
                                vmem_limit_bytes=VMEM_LIMIT)


def _mod_kernel(c_ref, w_ref, b_ref, o_ref):
    c = c_ref[...]
    s = c * jax.nn.sigmoid(c)
    o_ref[...] = jnp.dot(s, w_ref[...], preferred_element_type=F32,
                         precision=lax.Precision.HIGHEST) + b_ref[...]


def _modulation(cvec, w_ada, b_ada):
    n_out = w_ada.shape[1]
    tn = 1024
    return pl.pallas_call(
        _mod_kernel,
        out_shape=jax.ShapeDtypeStruct((MOD_ROWS, n_out), F32),
        grid=(n_out // tn,),
        in_specs=[pl.BlockSpec((MOD_ROWS, D_MODEL), lambda n: (0, 0)),
                  pl.BlockSpec((D_MODEL, tn), lambda n: (0, n)),
                  pl.BlockSpec((1, tn), lambda n: (0, n))],
        out_specs=pl.BlockSpec((MOD_ROWS, tn), lambda n: (0, n)),
        compiler_params=_params(1),
        name="modulation",
    )(cvec, w_ada, b_ada)


def _rope_tables(seq_len):
    rows = seq_len // GRID_W
    pos_r = np.repeat(np.arange(rows), GRID_W).astype(np.float64)
    pos_c = np.tile(np.arange(GRID_W), rows).astype(np.float64)

    def angles(d):
        n = d // 4
        inv = ROPE_BASE ** (-np.arange(n, dtype=np.float64) / n)
        return np.concatenate([pos_r[:, None] * inv, pos_c[:, None] * inv], axis=-1)

    ang_r = angles(RET_DK)
    cos_r = np.concatenate([np.cos(ang_r)] * 2, axis=-1)
    sin_r = np.concatenate([-np.sin(ang_r), np.sin(ang_r)], axis=-1)
    ang_d = angles(DIFF_DH)
    zero = np.zeros_like(ang_d)
    cos_d = np.concatenate([np.cos(ang_d)] * 4, axis=-1)
    sin_lo = np.concatenate([zero, np.sin(ang_d)] * 2, axis=-1)
    sin_hi = np.concatenate([-np.sin(ang_d), zero] * 2, axis=-1)
    return tuple(jnp.asarray(t, dtype=F32) for t in (cos_r, sin_r, cos_d, sin_lo, sin_hi))


def _inproj_kernel(*refs, latent, tm):
    if latent:
        (x_ref, sh_ref, sc_ref, g1_ref, w_ref, bg_ref,
         cr_ref, sr_ref, cd_ref, slo_ref, shi_ref, o_ref) = refs
    else:
        x_ref, sh_ref, sc_ref, g1_ref, w_ref, bg_ref, o_ref, ko_ref, vo_ref = refs
    heads = SEG // LANES

    def seg(n):
        return slice(n * SEG, (n + 1) * SEG)

    def normed(rs):
        x = x_ref[0, rs, :]
        ms = jnp.mean(x * x, axis=-1, keepdims=True)
        y = x * lax.rsqrt(ms + RMS_EPS) * g1_ref[...]
        return (y * (1.0 + sc_ref[0]) + sh_ref[0]).astype(BF)

    def rope_r(a, rs):
        return a * cr_ref[rs, :] + pltpu.roll(a, 64, 1) * sr_ref[rs, :]

    def rope_d(a, rs):
        return (a * cd_ref[rs, :] + pltpu.roll(a, 32, 1) * slo_ref[rs, :]
                + pltpu.roll(a, 96, 1) * shi_ref[rs, :])

    def per_head(rs, n, acc, fn):
        for j in range(heads):
            lo = n * SEG + j * LANES
            o_ref[0, rs, lo:lo + LANES] = fn(acc[:, j * LANES:(j + 1) * LANES], j).astype(BF)

    for r in range(tm // PROJ_ROWS):
        rs = slice(r * PROJ_ROWS, (r + 1) * PROJ_ROWS)
        h = normed(rs)

        def project(n):
            return jnp.dot(h, w_ref[:, seg(n)], preferred_element_type=F32)

        def q_or_k_r(a, j):
            a = rope_r(a, rs) if latent else a
            return a * K_R_SCALE if j >= heads // 2 else a

        per_head(rs, 0, project(0), q_or_k_r)
        o_ref[0, rs, seg(1)] = project(1).astype(BF)
        acc = project(2)
        o_ref[0, rs, seg(2)] = (acc * jax.nn.sigmoid(acc)).astype(BF)
        acc = project(3)
        if latent:
            per_head(rs, 3, acc, lambda a, j: rope_d(a, rs) * Q_D_SCALE)
        else:
            o_ref[0, rs, seg(3)] = (acc * Q_D_SCALE).astype(BF)
        acc = project(4)
        if latent:
            per_head(rs, 4, acc, lambda a, j: rope_d(a, rs))
        else:
            o_ref[0, rs, seg(4)] = acc.astype(BF)
            ko_ref[0, rs, :] = acc
        acc = project(5)
        o_ref[0, rs, seg(5)] = acc.astype(BF)
        if not latent:
            vo_ref[0, rs, :] = acc
        for n in (6, 7):
            gate_cols = slice((n - 6) * SEG, (n - 5) * SEG)
            o_ref[0, rs, seg(n)] = jax.nn.sigmoid(project(n) + bg_ref[:, gate_cols]).astype(BF)


def _inproj(x, mod3, norm_g, w_in_bf, b_gate, rope, *, latent, tm):
    bx, lx, _ = x.shape
    row = (lambda b: b) if latent else (lambda b: CTX_ROW)

    def mod_spec(k):
        return pl.BlockSpec((1, 1, D_MODEL), lambda b, m: (row(b), 0, k))

    def resident(shape):
        return pl.BlockSpec(shape, lambda b, m: (0, 0), pipeline_mode=pl.Buffered(1))

    in_specs = [pl.BlockSpec((1, tm, D_MODEL), lambda b, m: (b, m, 0)),
                mod_spec(0), mod_spec(1),
                resident(norm_g.shape), resident(w_in_bf.shape), resident(b_gate.shape)]
    args = [x, mod3, mod3, norm_g, w_in_bf, b_gate]
    proj_shape = jax.ShapeDtypeStruct((bx, lx, W_IN_COLS), BF)
    proj_spec = pl.BlockSpec((1, tm, W_IN_COLS), lambda b, m: (b, m, 0))
    if latent:
        in_specs += [pl.BlockSpec((tm, LANES), lambda b, m: (m, 0))] * 5
        args += list(rope)
        out_shape, out_specs = proj_shape, proj_spec
    else:
        kv_shape = jax.ShapeDtypeStruct((bx, lx, SEG), F32)
        kv_spec = pl.BlockSpec((1, tm, SEG), lambda b, m: (b, m, 0))
        out_shape, out_specs = (proj_shape, kv_shape, kv_shape), (proj_spec, kv_spec, kv_spec)
    return pl.pallas_call(
        functools.partial(_inproj_kernel, latent=latent, tm=tm),
        out_shape=out_shape,
        grid=(bx, lx // tm),
        in_specs=in_specs,
        out_specs=out_specs,
        compiler_params=_params(2),
        name="inproj_latent" if latent else "inproj_ctx",
    )(*args)


def _ret_kernel(*refs, n_chunks, hb, has_s0, emit_state):
    it = iter(refs)
    q_ref, k_ref, v_ref, g_ref, df_ref, db_ref, gn_ref = (next(it) for _ in range(7))
    s0f_ref, s0b_ref = (next(it), next(it)) if has_s0 else (None, None)
    y_ref = next(it)
    sfo_ref, sbo_ref = (next(it), next(it)) if emit_state else (None, None)
    sf_scr, sb_scr = next(it), next(it)
    C = RET_CHUNK
    unroll = max(1, min(8 // hb, n_chunks))
    lane = lax.broadcasted_iota(jnp.int32, (1, RET_HEADS), 1)
    lg_f_all = jax.nn.log_sigmoid(df_ref[...].astype(F32))
    lg_b_all = jax.nn.log_sigmoid(db_ref[...].astype(F32))
    ri = lax.broadcasted_iota(jnp.int32, (C, C), 0).astype(F32)
    ci = lax.broadcasted_iota(jnp.int32, (C, C), 1).astype(F32)
    dist = ri - ci
    rk = lax.broadcasted_iota(jnp.int32, (C, RET_DK), 0).astype(F32)

    def head_consts(j):
        head = pl.program_id(1) * hb + j
        lgf = jnp.sum(jnp.where(lane == head, lg_f_all, 0.0), axis=-1, keepdims=True)
        lgb = jnp.sum(jnp.where(lane == head, lg_b_all, 0.0), axis=-1, keepdims=True)
        decay = jnp.where(dist >= 0, jnp.exp(lgf * jnp.maximum(dist, 0.0)),
                          jnp.exp(lgb * jnp.maximum(-dist, 0.0)))
        return dict(decay=decay,
                    wq_f=jnp.exp(lgf * (rk + 1.0)), wq_b=jnp.exp(lgb * (C - rk)),
                    wk_f=jnp.exp(lgf * (C - 1.0 - rk)), wk_b=jnp.exp(lgb * rk),
                    gc_f=jnp.exp(lgf * C), gc_b=jnp.exp(lgb * C))

    consts = [head_consts(j) for j in range(hb)]

    def rows(c):
        return pl.ds(pl.multiple_of(c * C, C), C)

    def cols(j, width):
        return slice(j * width, (j + 1) * width)

    def kv_update(j, c, state, wk, gc):
        kw = (k_ref[0, rows(c), cols(j, RET_DK)].astype(F32) * wk).astype(BF)
        kv = lax.dot_general(kw, v_ref[0, rows(c), cols(j, RET_DV)], (((0,), (0,)), ((), ())),
                             preferred_element_type=F32)
        return gc * state + kv

    def scan_body(n, carry):
        cb = n_chunks - 1 - n
        new = []
        for j in range(hb):
            sf, sb = carry[2 * j], carry[2 * j + 1]
            sf_scr[j, n] = sf.astype(BF)
            sb_scr[j, cb] = sb.astype(BF)
            new.append(kv_update(j, n, sf, consts[j]["wk_f"], consts[j]["gc_f"]))
            new.append(kv_update(j, cb, sb, consts[j]["wk_b"], consts[j]["gc_b"]))
        return tuple(new)

    init = []
    for j in range(hb):
        if has_s0:
            init += [s0f_ref[j].astype(F32), s0b_ref[j].astype(F32)]
        else:
            init += [jnp.zeros((RET_DK, RET_DV), F32)] * 2
    final = lax.fori_loop(0, n_chunks, scan_body, tuple(init), unroll=unroll)
    if emit_state:
        for j in range(hb):
            sfo_ref[j] = final[2 * j]
            sbo_ref[j] = final[2 * j + 1]

    def out_body(c, _):
        for j in range(hb):
            k = consts[j]
            q = q_ref[0, rows(c), cols(j, RET_DK)]
            qf32 = q.astype(F32)
            v = v_ref[0, rows(c), cols(j, RET_DV)]
            sc = lax.dot_general(q, k_ref[0, rows(c), cols(j, RET_DK)], (((1,), (1,)), ((), ())),
                                 preferred_element_type=F32)
            o = jnp.dot((sc * k["decay"]).astype(BF), v, preferred_element_type=F32)
            o = o + jnp.dot((qf32 * k["wq_f"]).astype(BF), sf_scr[j, c], preferred_element_type=F32)
            o = o + jnp.dot((qf32 * k["wq_b"]).astype(BF), sb_scr[j, c], preferred_element_type=F32)
            mu = jnp.mean(o, axis=-1, keepdims=True)
            oc = o - mu
            var = jnp.mean(oc * oc, axis=-1, keepdims=True)
            y = oc * lax.rsqrt(var + GN_EPS) * gn_ref[:, cols(j, RET_DV)]
            gate = g_ref[0, rows(c), cols(j, RET_DV)].astype(F32)
            y_ref[0, rows(c), cols(j, RET_DV)] = (gate * y).astype(BF)
        return 0

    lax.fori_loop(0, n_chunks, out_body, 0, unroll=unroll)


def _retention(proj, decay_f, decay_b, gn_g, s0_f, s0_b, *, emit_state, hb):
    bx, lx, _ = proj.shape
    n_chunks = lx // RET_CHUNK
    has_s0 = s0_f is not None
    groups = RET_HEADS // hb
    k_off = (RET_HEADS * RET_DK) // (hb * RET_DK)
    v_off = (2 * RET_HEADS * RET_DK) // (hb * RET_DV)
    g_off = v_off + groups
    state_spec = pl.BlockSpec((None, None, hb, RET_DK, RET_DV), lambda b, h: (b, 0, h, 0, 0))
    in_specs = [pl.BlockSpec((1, lx, hb * RET_DK), lambda b, h: (b, 0, h)),
                pl.BlockSpec((1, lx, hb * RET_DK), lambda b, h: (b, 0, k_off + h)),
                pl.BlockSpec((1, lx, hb * RET_DV), lambda b, h: (b, 0, v_off + h)),
                pl.BlockSpec((1, lx, hb * RET_DV), lambda b, h: (b, 0, g_off + h)),
                pl.BlockSpec((1, RET_HEADS), lambda b, h: (0, 0)),
                pl.BlockSpec((1, RET_HEADS), lambda b, h: (0, 0)),
                pl.BlockSpec((1, hb * RET_DV), lambda b, h: (0, h))]
    args = [proj, proj, proj, proj, decay_f, decay_b, gn_g]
    if has_s0:
        in_specs += [state_spec, state_spec]
        args += [s0_f, s0_b]
    y_shape = jax.ShapeDtypeStruct((bx, lx, RET_HEADS * RET_DV), BF)
    y_spec = pl.BlockSpec((1, lx, hb * RET_DV), lambda b, h: (b, 0, h))
    if emit_state:
        st_shape = jax.ShapeDtypeStruct((bx, 1, RET_HEADS, RET_DK, RET_DV), F32)
        out_shape, out_specs = (y_shape, st_shape, st_shape), (y_spec, state_spec, state_spec)
    else:
        out_shape, out_specs = y_shape, y_spec
    snap = pltpu.VMEM((hb, n_chunks, RET_DK, RET_DV), BF)
    return pl.pallas_call(
        functools.partial(_ret_kernel, n_chunks=n_chunks, hb=hb, has_s0=has_s0, emit_state=emit_state),
        out_shape=out_shape,
        grid=(bx, groups),
        in_specs=in_specs,
        out_specs=out_specs,
        scratch_shapes=[snap, snap],
        compiler_params=_params(2),
        name="retention_latent" if has_s0 else "retention_ctx",
    )(*args)


def _attn_kernel(*refs, lx, lk_ctx, bq, hb):
    it = iter(refs)
    q_ref, k_ref, v_ref = next(it), next(it), next(it)
    kctx_ref, vctx_ref = (next(it), next(it)) if lk_ctx else (None, None)
    lq1_ref, lk1_ref, lq2_ref, lk2_ref, sg_ref = (next(it) for _ in range(5))
    o_ref = next(it)
    vt, s_scr, m_scr, acc_scr = next(it), next(it), next(it), next(it)
    lk = lx + lk_ctx
    n_kc = lk // ATT_KC
    n_q = lx // bq

    def cols(j):
        return slice(j * LANES, (j + 1) * LANES)

    def chunk(c):
        return slice(c * ATT_KC, (c + 1) * ATT_KC)

    def keys(j, c):
        lo = c * ATT_KC
        if lo < lx:
            return k_ref[0, lo:lo + ATT_KC, cols(j)]
        return kctx_ref[0, lo - lx:lo - lx + ATT_KC, cols(j)].astype(BF)

    for j in range(hb):
        for c in range(n_kc):
            lo = c * ATT_KC
            if lo < lx:
                vc = v_ref[0, lo:lo + ATT_KC, cols(j)].astype(F32)
            else:
                vc = vctx_ref[0, lo - lx:lo - lx + ATT_KC, cols(j)]
            vt[j, 0:LANES, chunk(c)] = vc.T.astype(BF)
        vt[j, LANES:ATT_VROWS, :] = jnp.ones((ATT_VROWS - LANES, lk), BF)

    def lam_term(a_ref, b_ref):
        return jnp.exp(jnp.sum(a_ref[...].astype(F32) * b_ref[...].astype(F32), axis=-1, keepdims=True))

    lam = lam_term(lq1_ref, lk1_ref) - lam_term(lq2_ref, lk2_ref) + LAM_INIT
    lane = lax.broadcasted_iota(jnp.int32, (bq, LANES), 1)

    def rows(t):
        return pl.ds(pl.multiple_of(t * bq, bq), bq)

    def scores(j, t, slot):
        q = q_ref[0, rows(t), cols(j)]
        zero = jnp.zeros_like(q)
        q_maps = (jnp.where(lane < DIFF_DH, q, zero), jnp.where(lane >= DIFF_DH, q, zero))
        m = [jnp.full((8, bq), -jnp.inf, F32) for _ in range(2)]
        for c in range(n_kc):
            kk = keys(j, c)
            for i in range(2):
                s = lax.dot_general(kk, q_maps[i], (((1,), (1,)), ((), ())),
                                    preferred_element_type=F32)
                s_scr[j, slot, i, chunk(c), :] = s
                m[i] = jnp.maximum(m[i], jnp.max(s.reshape(ATT_KC // 8, 8, bq), axis=0))
            yield
        for i in range(2):
            m_scr[j, slot, i] = jnp.broadcast_to(jnp.max(m[i], axis=0, keepdims=True), (8, bq))

    def values(j, slot):
        m = [m_scr[j, slot, i][0:1, :] for i in range(2)]
        acc = [jnp.zeros((ATT_VROWS, bq), F32) for _ in range(2)]
        for c in range(n_kc):
            for i in range(2):
                e = jnp.exp(s_scr[j, slot, i, chunk(c), :] - m[i]).astype(BF)
                acc[i] = acc[i] + jnp.dot(vt[j, :, chunk(c)], e, preferred_element_type=F32)
            yield
        for i in range(2):
            acc_scr[j, slot, i] = acc[i]

    def finish(j, t, slot):
        outs = [acc_scr[j, slot, i, 0:LANES, :] / acc_scr[j, slot, i, LANES:LANES + 1, :] for i in range(2)]
        ot = outs[0] - lam * outs[1]
        ms = jnp.mean(ot * ot, axis=0, keepdims=True)
        o = (ot * lax.rsqrt(ms + RMS_EPS)).T * sg_ref[...]
        o_ref[0, rows(t), cols(j)] = (o * (1.0 - LAM_INIT)).astype(BF)
        yield

    def chain(*streams):
        for g in streams:
            yield from g

    def run(*streams):
        live = list(streams)
        while live:
            live = [g for g in live if next(g, StopIteration) is not StopIteration]

    if n_q == 1:
        run(*[chain(scores(j, 0, 0), values(j, 0), finish(j, 0, 0)) for j in range(hb)])
        return
    assert hb == 1 and n_q % 2 == 0 and n_q >= 4
    run(scores(0, 0, 0))
    run(scores(0, 1, 1), values(0, 0))
    run(scores(0, 2, 0), chain(finish(0, 0, 0), values(0, 1)))

    def body(u, _):
        t = 2 * u
        run(scores(0, t + 1, 1), chain(finish(0, t - 1, 1), values(0, 0)))
        run(scores(0, t + 2, 0), chain(finish(0, t, 0), values(0, 1)))
        return 0

    lax.fori_loop(1, (n_q - 2) // 2, body, 0)
    run(scores(0, n_q - 1, 1), chain(finish(0, n_q - 3, 1), values(0, 0)))
    run(chain(finish(0, n_q - 2, 0), values(0, 1)))
    run(finish(0, n_q - 1, 1))


def _attention(proj, k_ctx, v_ctx, lam_params, subln_g, *, bq, hb):
    bx, lx, _ = proj.shape
    lk_ctx = 0 if k_ctx is None else k_ctx.shape[1]
    lk = lx + lk_ctx
    head_w = 2 * DIFF_DH
    width = hb * head_w
    q_off = 3 * SEG // width
    k_off = 4 * SEG // width
    v_off = 5 * SEG // width

    def heads(off):
        return pl.BlockSpec((1, lx, width), lambda b, h: (b, 0, off + h))

    in_specs = [heads(q_off), heads(k_off), heads(v_off)]
    args = [proj, proj, proj]
    if lk_ctx:
        ctx_spec = pl.BlockSpec((1, lk_ctx, width), lambda b, h: (b, 0, h))
        in_specs += [ctx_spec, ctx_spec]
        args += [k_ctx, v_ctx]
    small = pl.BlockSpec((1, DIFF_DH), lambda b, h: (0, 0))
    in_specs += [small] * 4 + [pl.BlockSpec((1, head_w), lambda b, h: (0, 0))]
    args += list(lam_params) + [subln_g]
    return pl.pallas_call(
        functools.partial(_attn_kernel, lx=lx, lk_ctx=lk_ctx, bq=bq, hb=hb),
        out_shape=jax.ShapeDtypeStruct((bx, lx, DIFF_HEADS * head_w), BF),
        grid=(bx, DIFF_HEADS // hb),
        in_specs=in_specs,
        out_specs=heads(0),
        scratch_shapes=[pltpu.VMEM((hb, ATT_VROWS, lk), BF),
                        pltpu.VMEM((hb, 2, 2, lk, bq), F32), pltpu.VMEM((hb, 2, 2, 8, bq), F32),
                        pltpu.VMEM((hb, 2, 2, ATT_VROWS, bq), F32)],
        compiler_params=_params(2),
        name="diff_attn_latent" if lk_ctx else "diff_attn_ctx",
    )(*args)


def _post_kernel(x_ref, yr_ref, od_ref, gr_ref, gd_ref, gate1_ref, sh2_ref, sc2_ref, gate2_ref,
                 n2_ref, fn_ref, wr_ref, wd_ref, wo_ref, wg_ref, wu_ref, wdn_ref, o_ref):
    br = jnp.dot(yr_ref[0], wr_ref[...], preferred_element_type=F32)
    bd = jnp.dot(od_ref[0], wd_ref[...], preferred_element_type=F32)
    merged = (gr_ref[0].astype(F32) * br + gd_ref[0].astype(F32) * bd).astype(BF)
    mix = jnp.dot(merged, wo_ref[...], preferred_element_type=F32)
    x1 = x_ref[0] + gate1_ref[0] * mix
    ms = jnp.mean(x1 * x1, axis=-1, keepdims=True)
    h2 = x1 * lax.rsqrt(ms + RMS_EPS) * n2_ref[...]
    h2 = (h2 * (1.0 + sc2_ref[0]) + sh2_ref[0]).astype(BF)
    ffn = None
    for lo, hi in FFN_SPLITS:
        g = jnp.dot(h2, wg_ref[:, lo:hi], preferred_element_type=F32)
        u = jnp.dot(h2, wu_ref[:, lo:hi], preferred_element_type=F32)
        act = (g * jax.nn.sigmoid(g) * u).astype(BF)
        part = jnp.dot(act, wdn_ref[lo:hi, :], preferred_element_type=F32)
        ffn = part if ffn is None else ffn + part
    x2 = x1 + gate2_ref[0] * ffn
    ms2 = jnp.mean(x2 * x2, axis=-1, keepdims=True)
    o_ref[0] = x2 * lax.rsqrt(ms2 + RMS_EPS) * fn_ref[...]


def _post(x, y_r, o_d, proj, mod3, norm2_g, final_g, weights, *, latent, tm):
    bx, lx, _ = x.shape
    row = (lambda b: b) if latent else (lambda b: CTX_ROW)

    def tok(width, col):
        return pl.BlockSpec((1, tm, width), lambda b, m: (b, m, col))

    def mod_spec(k):
        return pl.BlockSpec((1, 1, D_MODEL), lambda b, m: (row(b), 0, k))

    def resident(shape):
        return pl.BlockSpec(shape, lambda b, m: (0, 0), pipeline_mode=pl.Buffered(1))

    vec = pl.BlockSpec((1, D_MODEL), lambda b, m: (0, 0))
    in_specs = [tok(D_MODEL, 0), tok(D_MODEL, 0), tok(D_MODEL, 0), tok(SEG, 6), tok(SEG, 7),
                mod_spec(2), mod_spec(3), mod_spec(4), mod_spec(5), vec, vec]
    in_specs += [resident(w.shape) for w in weights]
    return pl.pallas_call(
        _post_kernel,
        out_shape=jax.ShapeDtypeStruct((bx, lx, D_MODEL), F32),
        grid=(bx, lx // tm),
        in_specs=in_specs,
        out_specs=tok(D_MODEL, 0),
        compiler_params=_params(2),
        name="post_latent" if latent else "post_ctx",
    )(x, y_r, o_d, proj, proj, mod3, mod3, mod3, mod3, norm2_g, final_g, *weights)


def kernel(x_prompt, x_sample, state_ret_fwd, state_ret_bwd, cache_diff_k, cache_diff_v, c, c_ctx,
           norm1_g, norm2_g, w_ada, b_ada, w_in, b_gate, ret_decay_fwd, ret_decay_bwd, ret_gn_g,
           w_ret_out, diff_lambda_q1, diff_lambda_k1, diff_lambda_q2, diff_lambda_k2, diff_subln_g,
           w_diff_out, w_o, w_ffn_gate, w_ffn_up, w_ffn_down, final_norm_g):
    batch, seq, _ = x_prompt.shape
    dec_batch, dec_seq, _ = x_sample.shape
    past_len = cache_diff_k.shape[2]
    layer = 0

    cvec = jnp.zeros((MOD_ROWS, D_MODEL), F32).at[:dec_batch].set(c).at[CTX_ROW].set(c_ctx)
    mod = _modulation(cvec, w_ada[layer], b_ada[layer][None, :])
    mod3 = mod.reshape(MOD_ROWS, 1, 6 * D_MODEL)

    w_in_bf = w_in[layer].astype(BF)
    post_w = tuple(w[layer].astype(BF) for w in (w_ret_out, w_diff_out, w_o, w_ffn_gate, w_ffn_up, w_ffn_down))
    n1 = norm1_g[layer][None, :]
    n2 = norm2_g[layer][None, :]
    fn = final_norm_g[None, :]
    bg = b_gate[layer][None, :]
    dec_f = ret_decay_fwd[layer][None, :]
    dec_b = ret_decay_bwd[layer][None, :]
    gn = ret_gn_g[layer][None, :]
    lam_params = tuple(p[layer][None, :] for p in (diff_lambda_q1, diff_lambda_k1, diff_lambda_q2, diff_lambda_k2))
    subln = diff_subln_g[layer][None, :]

    xp_flat = x_prompt.reshape(1, batch * seq, D_MODEL)
    proj_c, k_new, v_new = _inproj(xp_flat, mod3, n1, w_in_bf, bg, None, latent=False, tm=512)
    proj_c_b = proj_c.reshape(batch, seq, W_IN_COLS)
    yr_c, s_f, s_b = _retention(proj_c_b, dec_f, dec_b, gn, None, None, emit_state=True, hb=RET_HEADS)
    od_c = _attention(proj_c_b, None, None, lam_params, subln, bq=seq, hb=DIFF_HEADS)
    y_prompt = _post(xp_flat, yr_c.reshape(1, batch * seq, -1), od_c.reshape(1, batch * seq, -1),
                     proj_c, mod3, n2, fn, post_w, latent=False, tm=512).reshape(batch, seq, D_MODEL)

    rope = _rope_tables(dec_seq)
    proj_l = _inproj(x_sample, mod3, n1, w_in_bf, bg, rope, latent=True, tm=512)
    yr_l = _retention(proj_l, dec_f, dec_b, gn, state_ret_fwd, state_ret_bwd, emit_state=False, hb=2)
    k_ctx = cache_diff_k[:, layer].reshape(dec_batch, past_len, DIFF_HEADS * 2 * DIFF_DH)
    v_ctx = cache_diff_v[:, layer].reshape(dec_batch, past_len, DIFF_HEADS * 2 * DIFF_DH)
    od_l = _attention(proj_l, k_ctx, v_ctx, lam_params, subln, bq=256, hb=1)
    y_sample = _post(x_sample, yr_l, od_l, proj_l, mod3, n2, fn, post_w, latent=True, tm=512)

    new_diff_k = k_new.reshape(batch, 1, seq, DIFF_HEADS, 2, DIFF_DH)
    new_diff_v = v_new.reshape(batch, 1, seq, DIFF_HEADS, 2 * DIFF_DH)
    return (y_prompt, y_sample, s_f, s_b, new_diff_k, new_diff_v)
```

```python
import functools
import math

import jax
import jax.numpy as jnp
import numpy as np
from jax import lax
from jax.experimental import pallas as pl
from jax.experimental.pallas import tpu as pltpu

D_MODEL = 1024
GRID_W = 64
ROPE_BASE = 10000.0
RET_HEADS = 4
RET_DK = 128
RET_DV = 256
RET_CHUNK = 128
DIFF_HEADS = 8
DIFF_DH = 64
FFN_HIDDEN = 2816
W_IN_COLS = 8192
RMS_EPS = 1e-6
GN_EPS = 1e-5
LAM_INIT = 0.8 - 0.6 * math.exp(-0.3 * 0)

SEG = 1024
N_SEG = W_IN_COLS // SEG
LANES = 128
MOD_ROWS = 8
CTX_ROW = 4
K_R_SCALE = RET_DK ** -0.5
Q_D_SCALE = DIFF_DH ** -0.5
FFN_SPLITS = ((0, 1536), (1536, FFN_HIDDEN))
PROJ_ROWS = 256
ATT_KC = 256
ATT_VROWS = LANES + 16

BF = jnp.bfloat16
F32 = jnp.float32
VMEM_LIMIT = 56 * 1024 * 1024


def _params(n_axes):
    return pltpu.CompilerParams(dimension_semantics=("arbitrary",) * n_axes,
                                vmem_limit_bytes=VMEM_LIMIT)


def _mod_kernel(c_ref, w_ref, b_ref, o_ref):
    c = c_ref[...]
    s = c * jax.nn.sigmoid(c)
    o_ref[...] = jnp.dot(s, w_ref[...], preferred_element_type=F32,
                         precision=lax.Precision.HIGHEST) + b_ref[...]


def _modulation(cvec, w_ada, b_ada):
    n_out = w_ada.shape[1]
    tn = 1024
    return pl.pallas_call(
        _mod_kernel,
        out_shape=jax.ShapeDtypeStruct((MOD_ROWS, n_out), F32),
        grid=(n_out // tn,),
        in_specs=[pl.BlockSpec((MOD_ROWS, D_MODEL), lambda n: (0, 0)),
                  pl.BlockSpec((D_MODEL, tn), lambda n: (0, n)),
                  pl.BlockSpec((1, tn), lambda n: (0, n))],
        out_specs=pl.BlockSpec((MOD_ROWS, tn), lambda n: (0, n)),
        compiler_params=_params(1),
        name="modulation",
    )(cvec, w_ada, b_ada)


def _rope_tables(seq_len):
    rows = seq_len // GRID_W
    pos_r = np.repeat(np.arange(rows), GRID_W).astype(np.float64)
    pos_c = np.tile(np.arange(GRID_W), rows).astype(np.float64)

    def angles(d):
        n = d // 4
        inv = ROPE_BASE ** (-np.arange(n, dtype=np.float64) / n)
        return np.concatenate([pos_r[:, None] * inv, pos_c[:, None] * inv], axis=-1)

    ang_r = angles(RET_DK)
    cos_r = np.concatenate([np.cos(ang_r)] * 2, axis=-1)
    sin_r = np.concatenate([-np.sin(ang_r), np.sin(ang_r)], axis=-1)
    ang_d = angles(DIFF_DH)
    zero = np.zeros_like(ang_d)
    cos_d = np.concatenate([np.cos(ang_d)] * 4, axis=-1)
    sin_lo = np.concatenate([zero, np.sin(ang_d)] * 2, axis=-1)
    sin_hi = np.concatenate([-np.sin(ang_d), zero] * 2, axis=-1)
    return tuple(jnp.asarray(t, dtype=F32) for t in (cos_r, sin_r, cos_d, sin_lo, sin_hi))


def _inproj_kernel(*refs, latent, tm):
    if latent:
        (x_ref, sh_ref, sc_ref, g1_ref, w_ref, bg_ref,
         cr_ref, sr_ref, cd_ref, slo_ref, shi_ref, o_ref) = refs
    else:
        x_ref, sh_ref, sc_ref, g1_ref, w_ref, bg_ref, o_ref, ko_ref, vo_ref = refs
    heads = SEG // LANES

    def seg(n):
        return slice(n * SEG, (n + 1) * SEG)

    def normed(rs):
        x = x_ref[0, rs, :]
        ms = jnp.mean(x * x, axis=-1, keepdims=True)
        y = x * lax.rsqrt(ms + RMS_EPS) * g1_ref[...]
        return (y * (1.0 + sc_ref[0]) + sh_ref[0]).astype(BF)

    def rope_r(a, rs):
        return a * cr_ref[rs, :] + pltpu.roll(a, 64, 1) * sr_ref[rs, :]

    def rope_d(a, rs):
        return (a * cd_ref[rs, :] + pltpu.roll(a, 32, 1) * slo_ref[rs, :]
                + pltpu.roll(a, 96, 1) * shi_ref[rs, :])

    def per_head(rs, n, acc, fn):
        for j in range(heads):
            lo = n * SEG + j * LANES
            o_ref[0, rs, lo:lo + LANES] = fn(acc[:, j * LANES:(j + 1) * LANES], j).astype(BF)

    for r in range(tm // PROJ_ROWS):
        rs = slice(r * PROJ_ROWS, (r + 1) * PROJ_ROWS)
        h = normed(rs)

        def project(n):
            return jnp.dot(h, w_ref[:, seg(n)], preferred_element_type=F32)

        def q_or_k_r(a, j):
            a = rope_r(a, rs) if latent else a
            return a * K_R_SCALE if j >= heads // 2 else a

        per_head(rs, 0, project(0), q_or_k_r)
        o_ref[0, rs, seg(1)] = project(1).astype(BF)
        acc = project(2)
        o_ref[0, rs, seg(2)] = (acc * jax.nn.sigmoid(acc)).astype(BF)
        acc = project(3)
        if latent:
            per_head(rs, 3, acc, lambda a, j: rope_d(a, rs) * Q_D_SCALE)
        else:
            o_ref[0, rs, seg(3)] = (acc * Q_D_SCALE).astype(BF)
        acc = project(4)
        if latent:
            per_head(rs, 4, acc, lambda a, j: rope_d(a, rs))
        else:
            o_ref[0, rs, seg(4)] = acc.astype(BF)
            ko_ref[r] = acc.T
        acc = project(5)
        o_ref[0, rs, seg(5)] = acc.astype(BF)
        if not latent:
            vo_ref[0, rs, :] = acc
        for n in (6, 7):
            gate_cols = slice((n - 6) * SEG, (n - 5) * SEG)
            o_ref[0, rs, seg(n)] = jax.nn.sigmoid(project(n) + bg_ref[:, gate_cols]).astype(BF)


def _inproj(x, mod3, norm_g, w_in_bf, b_gate, rope, *, latent, tm):
    bx, lx, _ = x.shape
    row = (lambda b: b) if latent else (lambda b: CTX_ROW)

    def mod_spec(k):
        return pl.BlockSpec((1, 1, D_MODEL), lambda b, m: (row(b), 0, k))

    def resident(shape):
        return pl.BlockSpec(shape, lambda b, m: (0, 0), pipeline_mode=pl.Buffered(1))

    in_specs = [pl.BlockSpec((1, tm, D_MODEL), lambda b, m: (b, m, 0)),
                mod_spec(0), mod_spec(1),
                resident(norm_g.shape), resident(w_in_bf.shape), resident(b_gate.shape)]
    args = [x, mod3, mod3, norm_g, w_in_bf, b_gate]
    proj_shape = jax.ShapeDtypeStruct((bx, lx, W_IN_COLS), BF)
    proj_spec = pl.BlockSpec((1, tm, W_IN_COLS), lambda b, m: (b, m, 0))
    if latent:
        in_specs += [pl.BlockSpec((tm, LANES), lambda b, m: (m, 0))] * 5
        args += list(rope)
        out_shape, out_specs = proj_shape, proj_spec
    else:
        v_shape = jax.ShapeDtypeStruct((bx, lx, SEG), F32)
        v_spec = pl.BlockSpec((1, tm, SEG), lambda b, m: (b, m, 0))
        kt_shape = jax.ShapeDtypeStruct((bx * lx // PROJ_ROWS, SEG, PROJ_ROWS), F32)
        kt_spec = pl.BlockSpec((tm // PROJ_ROWS, SEG, PROJ_ROWS), lambda b, m: (b * (lx // tm) + m, 0, 0))
        out_shape, out_specs = (proj_shape, kt_shape, v_shape), (proj_spec, kt_spec, v_spec)
    return pl.pallas_call(
        functools.partial(_inproj_kernel, latent=latent, tm=tm),
        out_shape=out_shape,
        grid=(bx, lx // tm),
        in_specs=in_specs,
        out_specs=out_specs,
        compiler_params=_params(2),
        name="inproj_latent" if latent else "inproj_ctx",
    )(*args)


def _ret_unroll(hb, n_chunks):
    return max(1, min(8 // hb, n_chunks))


def _ret_kernel(*refs, n_chunks, hb, has_s0, emit_state):
    it = iter(refs)
    q_ref, k_ref, v_ref, g_ref, df_ref, db_ref, gn_ref = (next(it) for _ in range(7))
    s0f_ref, s0b_ref = (next(it), next(it)) if has_s0 else (None, None)
    y_ref = next(it)
    sfo_ref, sbo_ref = (next(it), next(it)) if emit_state else (None, None)
    sf_scr, sb_scr, o_scr = next(it), next(it), next(it)
    C = RET_CHUNK
    unroll = _ret_unroll(hb, n_chunks)
    lane = lax.broadcasted_iota(jnp.int32, (1, RET_HEADS), 1)
    lg_f_all = jax.nn.log_sigmoid(df_ref[...].astype(F32))
    lg_b_all = jax.nn.log_sigmoid(db_ref[...].astype(F32))
    ri = lax.broadcasted_iota(jnp.int32, (C, C), 0).astype(F32)
    ci = lax.broadcasted_iota(jnp.int32, (C, C), 1).astype(F32)
    dist = ri - ci
    rk = lax.broadcasted_iota(jnp.int32, (C, RET_DK), 0).astype(F32)

    def head_consts(j):
        head = pl.program_id(1) * hb + j
        lgf = jnp.sum(jnp.where(lane == head, lg_f_all, 0.0), axis=-1, keepdims=True)
        lgb = jnp.sum(jnp.where(lane == head, lg_b_all, 0.0), axis=-1, keepdims=True)
        decay = jnp.where(dist >= 0, jnp.exp(lgf * jnp.maximum(dist, 0.0)),
                          jnp.exp(lgb * jnp.maximum(-dist, 0.0)))
        return dict(decay=decay,
                    wq_f=jnp.exp(lgf * (rk + 1.0)), wq_b=jnp.exp(lgb * (C - rk)),
                    wk_f=jnp.exp(lgf * (C - 1.0 - rk)), wk_b=jnp.exp(lgb * rk),
                    gc_f=jnp.exp(lgf * C), gc_b=jnp.exp(lgb * C))

    consts = [head_consts(j) for j in range(hb)]

    def rows(c):
        return pl.ds(pl.multiple_of(c * C, C), C)

    def cols(j, width):
        return slice(j * width, (j + 1) * width)

    def run(streams):
        live = list(streams)
        while live:
            live = [g for g in live if next(g, StopIteration) is not StopIteration]

    def kv_chunk(j, c, wk):
        kw = (k_ref[0, rows(c), cols(j, RET_DK)].astype(F32) * wk).astype(BF)
        return lax.dot_general(kw, v_ref[0, rows(c), cols(j, RET_DV)], (((0,), (0,)), ((), ())),
                               preferred_element_type=F32)

    def scan_body(g, carry):
        state = list(carry)
        kvs = {}

        def products(i):
            n = g * unroll + i
            for j in range(hb):
                kvs[i, j] = (kv_chunk(j, n, consts[j]["wk_f"]), kv_chunk(j, n_chunks - 1 - n, consts[j]["wk_b"]))

        def updates(i):
            n = g * unroll + i
            for j in range(hb):
                sf, sb = state[2 * j], state[2 * j + 1]
                sf_scr[j, n] = sf.astype(BF)
                sb_scr[j, n_chunks - 1 - n] = sb.astype(BF)
                state[2 * j] = consts[j]["gc_f"] * sf + kvs[i, j][0]
                state[2 * j + 1] = consts[j]["gc_b"] * sb + kvs[i, j][1]

        products(0)
        for i in range(1, unroll):
            products(i)
            updates(i - 1)
        updates(unroll - 1)
        return tuple(state)

    init = []
    for j in range(hb):
        if has_s0:
            init += [s0f_ref[j].astype(F32), s0b_ref[j].astype(F32)]
        else:
            init += [jnp.zeros((RET_DK, RET_DV), F32)] * 2
    final = lax.fori_loop(0, n_chunks // unroll, scan_body, tuple(init))
    if emit_state:
        for j in range(hb):
            sfo_ref[j] = final[2 * j]
            sbo_ref[j] = final[2 * j + 1]

    def mix_chunk(j, c, buf, idx):
        k = consts[j]
        q = q_ref[0, rows(c), cols(j, RET_DK)]
        qf32 = q.astype(F32)
        v = v_ref[0, rows(c), cols(j, RET_DV)]
        sc = lax.dot_general(q, k_ref[0, rows(c), cols(j, RET_DK)], (((1,), (1,)), ((), ())),
                             preferred_element_type=F32)
        yield
        o = jnp.dot((sc * k["decay"]).astype(BF), v, preferred_element_type=F32)
        o = o + jnp.dot((qf32 * k["wq_f"]).astype(BF), sf_scr[j, c], preferred_element_type=F32)
        o = o + jnp.dot((qf32 * k["wq_b"]).astype(BF), sb_scr[j, c], preferred_element_type=F32)
        o_scr[buf, idx] = o

    def norm_chunk(j, c, buf, idx):
        o = o_scr[buf, idx]
        mu = jnp.mean(o, axis=-1, keepdims=True)
        yield
        oc = o - mu
        var = jnp.mean(oc * oc, axis=-1, keepdims=True)
        yield
        y = oc * lax.rsqrt(var + GN_EPS) * gn_ref[:, cols(j, RET_DV)]
        gate = g_ref[0, rows(c), cols(j, RET_DV)].astype(F32)
        y_ref[0, rows(c), cols(j, RET_DV)] = (gate * y).astype(BF)

    def group(fn, g, buf):
        return [fn(j, g * unroll + i, buf, i * hb + j) for i in range(unroll) for j in range(hb)]

    n_groups = n_chunks // unroll
    run(group(mix_chunk, 0, 0))
    if n_groups > 1:
        assert n_groups % 2 == 0
        run(group(mix_chunk, 1, 1) + group(norm_chunk, 0, 0))

        def out_body(u, _):
            run(group(mix_chunk, 2 * u, 0) + group(norm_chunk, 2 * u - 1, 1))
            run(group(mix_chunk, 2 * u + 1, 1) + group(norm_chunk, 2 * u, 0))
            return 0

        lax.fori_loop(1, n_groups // 2, out_body, 0)
    run(group(norm_chunk, n_groups - 1, (n_groups - 1) % 2))


def _retention(proj, decay_f, decay_b, gn_g, s0_f, s0_b, *, emit_state, hb):
    bx, lx, _ = proj.shape
    n_chunks = lx // RET_CHUNK
    has_s0 = s0_f is not None
    groups = RET_HEADS // hb
    k_off = (RET_HEADS * RET_DK) // (hb * RET_DK)
    v_off = (2 * RET_HEADS * RET_DK) // (hb * RET_DV)
    g_off = v_off + groups
    state_spec = pl.BlockSpec((None, None, hb, RET_DK, RET_DV), lambda b, h: (b, 0, h, 0, 0))
    in_specs = [pl.BlockSpec((1, lx, hb * RET_DK), lambda b, h: (b, 0, h)),
                pl.BlockSpec((1, lx, hb * RET_DK), lambda b, h: (b, 0, k_off + h)),
                pl.BlockSpec((1, lx, hb * RET_DV), lambda b, h: (b, 0, v_off + h)),
                pl.BlockSpec((1, lx, hb * RET_DV), lambda b, h: (b, 0, g_off + h)),
                pl.BlockSpec((1, RET_HEADS), lambda b, h: (0, 0)),
                pl.BlockSpec((1, RET_HEADS), lambda b, h: (0, 0)),
                pl.BlockSpec((1, hb * RET_DV), lambda b, h: (0, h))]
    args = [proj, proj, proj, proj, decay_f, decay_b, gn_g]
    if has_s0:
        in_specs += [state_spec, state_spec]
        args += [s0_f, s0_b]
    y_shape = jax.ShapeDtypeStruct((bx, lx, RET_HEADS * RET_DV), BF)
    y_spec = pl.BlockSpec((1, lx, hb * RET_DV), lambda b, h: (b, 0, h))
    if emit_state:
        st_shape = jax.ShapeDtypeStruct((bx, 1, RET_HEADS, RET_DK, RET_DV), F32)
        out_shape, out_specs = (y_shape, st_shape, st_shape), (y_spec, state_spec, state_spec)
    else:
        out_shape, out_specs = y_shape, y_spec
    snap = pltpu.VMEM((hb, n_chunks, RET_DK, RET_DV), BF)
    return pl.pallas_call(
        functools.partial(_ret_kernel, n_chunks=n_chunks, hb=hb, has_s0=has_s0, emit_state=emit_state),
        out_shape=out_shape,
        grid=(bx, groups),
        in_specs=in_specs,
        out_specs=out_specs,
        scratch_shapes=[snap, snap,
                        pltpu.VMEM((2, _ret_unroll(hb, n_chunks) * hb, RET_CHUNK, RET_DV), F32)],
        compiler_params=_params(2),
        name="retention_latent" if has_s0 else "retention_ctx",
    )(*args)


def _attn_kernel(*refs, lx, lk_ctx, bq, hb):
    it = iter(refs)
    q_ref, k_ref, v_ref = next(it), next(it), next(it)
    kctx_ref, vctx_ref = (next(it), next(it)) if lk_ctx else (None, None)
    lq1_ref, lk1_ref, lq2_ref, lk2_ref, sg_ref = (next(it) for _ in range(5))
    o_ref = next(it)
    vt, s_scr, m_scr, acc_scr = next(it), next(it), next(it), next(it)
    kctx_scr = next(it) if lk_ctx else None
    lk = lx + lk_ctx
    n_kc = lk // ATT_KC
    n_q = lx // bq

    def cols(j):
        return slice(j * LANES, (j + 1) * LANES)

    def chunk(c):
        return slice(c * ATT_KC, (c + 1) * ATT_KC)

    def keys(j, c):
        lo = c * ATT_KC
        if lo < lx:
            return k_ref[0, lo:lo + ATT_KC, cols(j)]
        return kctx_scr[j, lo - lx:lo - lx + ATT_KC, :]

    for j in range(hb):
        if lk_ctx:
            kctx_scr[j] = kctx_ref[0, cols(j), :].T.astype(BF)
        for c in range(n_kc):
            lo = c * ATT_KC
            if lo < lx:
                vc = v_ref[0, lo:lo + ATT_KC, cols(j)].astype(F32)
            else:
                vc = vctx_ref[0, lo - lx:lo - lx + ATT_KC, cols(j)]
            vt[j, 0:LANES, chunk(c)] = vc.T.astype(BF)
        vt[j, LANES:ATT_VROWS, :] = jnp.ones((ATT_VROWS - LANES, lk), BF)

    def lam_term(a_ref, b_ref):
        return jnp.exp(jnp.sum(a_ref[...].astype(F32) * b_ref[...].astype(F32), axis=-1, keepdims=True))

    lam = lam_term(lq1_ref, lk1_ref) - lam_term(lq2_ref, lk2_ref) + LAM_INIT
    lane = lax.broadcasted_iota(jnp.int32, (bq, LANES), 1)

    def rows(t):
        return pl.ds(pl.multiple_of(t * bq, bq), bq)

    def scores(j, t, slot):
        q = q_ref[0, rows(t), cols(j)]
        zero = jnp.zeros_like(q)
        q_maps = (jnp.where(lane < DIFF_DH, q, zero), jnp.where(lane >= DIFF_DH, q, zero))
        m = [jnp.full((8, bq), -jnp.inf, F32) for _ in range(2)]
        for c in range(n_kc):
            kk = keys(j, c)
            for i in range(2):
                s = lax.dot_general(kk, q_maps[i], (((1,), (1,)), ((), ())),
                                    preferred_element_type=F32)
                s_scr[j, slot, i, chunk(c), :] = s
                m[i] = jnp.maximum(m[i], jnp.max(s.reshape(ATT_KC // 8, 8, bq), axis=0))
            yield
        for i in range(2):
            m_scr[j, slot, i] = jnp.broadcast_to(jnp.max(m[i], axis=0, keepdims=True), (8, bq))

    def values(j, slot):
        m = [m_scr[j, slot, i][0:1, :] for i in range(2)]
        acc = [jnp.zeros((ATT_VROWS, bq), F32) for _ in range(2)]
        for c in range(n_kc):
            for i in range(2):
                e = jnp.exp(s_scr[j, slot, i, chunk(c), :] - m[i]).astype(BF)
                acc[i] = acc[i] + jnp.dot(vt[j, :, chunk(c)], e, preferred_element_type=F32)
            yield
        for i in range(2):
            acc_scr[j, slot, i] = acc[i]

    def finish(j, t, slot):
        outs = [acc_scr[j, slot, i, 0:LANES, :] / acc_scr[j, slot, i, LANES:LANES + 1, :] for i in range(2)]
        ot = outs[0] - lam * outs[1]
        ms = jnp.mean(ot * ot, axis=0, keepdims=True)
        o = (ot * lax.rsqrt(ms + RMS_EPS)).T * sg_ref[...]
        o_ref[0, rows(t), cols(j)] = (o * (1.0 - LAM_INIT)).astype(BF)
        yield

    def chain(*streams):
        for g in streams:
            yield from g

    def run(*streams):
        live = list(streams)
        while live:
            live = [g for g in live if next(g, StopIteration) is not StopIteration]

    if n_q == 1:
        run(*[chain(scores(j, 0, 0), values(j, 0), finish(j, 0, 0)) for j in range(hb)])
        return
    assert hb == 1 and n_q % 2 == 0 and n_q >= 4
    run(scores(0, 0, 0))
    run(scores(0, 1, 1), values(0, 0))
    run(scores(0, 2, 0), chain(finish(0, 0, 0), values(0, 1)))

    def body(u, _):
        t = 2 * u
        run(scores(0, t + 1, 1), chain(finish(0, t - 1, 1), values(0, 0)))
        run(scores(0, t + 2, 0), chain(finish(0, t, 0), values(0, 1)))
        return 0

    lax.fori_loop(1, (n_q - 2) // 2, body, 0)
    run(scores(0, n_q - 1, 1), chain(finish(0, n_q - 3, 1), values(0, 0)))
    run(chain(finish(0, n_q - 2, 0), values(0, 1)))
    run(finish(0, n_q - 1, 1))


def _attention(proj, kt_ctx, v_ctx, lam_params, subln_g, *, bq, hb):
    bx, lx, _ = proj.shape
    lk_ctx = 0 if kt_ctx is None else kt_ctx.shape[2]
    lk = lx + lk_ctx
    head_w = 2 * DIFF_DH
    width = hb * head_w
    q_off = 3 * SEG // width
    k_off = 4 * SEG // width
    v_off = 5 * SEG // width

    def heads(off):
        return pl.BlockSpec((1, lx, width), lambda b, h: (b, 0, off + h))

    in_specs = [heads(q_off), heads(k_off), heads(v_off)]
    args = [proj, proj, proj]
    scratch = [pltpu.VMEM((hb, ATT_VROWS, lk), BF),
               pltpu.VMEM((hb, 2, 2, lk, bq), F32), pltpu.VMEM((hb, 2, 2, 8, bq), F32),
               pltpu.VMEM((hb, 2, 2, ATT_VROWS, bq), F32)]
    if lk_ctx:
        assert lk_ctx % ATT_KC == 0
        in_specs += [pl.BlockSpec((1, width, lk_ctx), lambda b, h: (b, h, 0)),
                     pl.BlockSpec((1, lk_ctx, width), lambda b, h: (b, 0, h))]
        args += [kt_ctx, v_ctx]
        scratch.append(pltpu.VMEM((hb, lk_ctx, head_w), BF))
    small = pl.BlockSpec((1, DIFF_DH), lambda b, h: (0, 0))
    in_specs += [small] * 4 + [pl.BlockSpec((1, head_w), lambda b, h: (0, 0))]
    args += list(lam_params) + [subln_g]
    return pl.pallas_call(
        functools.partial(_attn_kernel, lx=lx, lk_ctx=lk_ctx, bq=bq, hb=hb),
        out_shape=jax.ShapeDtypeStruct((bx, lx, DIFF_HEADS * head_w), BF),
        grid=(bx, DIFF_HEADS // hb),
        in_specs=in_specs,
        out_specs=heads(0),
        scratch_shapes=scratch,
        compiler_params=_params(2),
        name="diff_attn_latent" if lk_ctx else "diff_attn_ctx",
    )(*args)


def _post_kernel(x_ref, yr_ref, od_ref, gr_ref, gd_ref, gate1_ref, sh2_ref, sc2_ref, gate2_ref,
                 n2_ref, fn_ref, wr_ref, wd_ref, wo_ref, wg_ref, wu_ref, wdn_ref, o_ref):
    br = jnp.dot(yr_ref[0], wr_ref[...], preferred_element_type=F32)
    bd = jnp.dot(od_ref[0], wd_ref[...], preferred_element_type=F32)
    merged = (gr_ref[0].astype(F32) * br + gd_ref[0].astype(F32) * bd).astype(BF)
    mix = jnp.dot(merged, wo_ref[...], preferred_element_type=F32)
    x1 = x_ref[0] + gate1_ref[0] * mix
    ms = jnp.mean(x1 * x1, axis=-1, keepdims=True)
    h2 = x1 * lax.rsqrt(ms + RMS_EPS) * n2_ref[...]
    h2 = (h2 * (1.0 + sc2_ref[0]) + sh2_ref[0]).astype(BF)
    ffn = None
    for lo, hi in FFN_SPLITS:
        g = jnp.dot(h2, wg_ref[:, lo:hi], preferred_element_type=F32)
        u = jnp.dot(h2, wu_ref[:, lo:hi], preferred_element_type=F32)
        act = (g * jax.nn.sigmoid(g) * u).astype(BF)
        part = jnp.dot(act, wdn_ref[lo:hi, :], preferred_element_type=F32)
        ffn = part if ffn is None else ffn + part
    x2 = x1 + gate2_ref[0] * ffn
    ms2 = jnp.mean(x2 * x2, axis=-1, keepdims=True)
    o_ref[0] = x2 * lax.rsqrt(ms2 + RMS_EPS) * fn_ref[...]


def _post(x, y_r, o_d, proj, mod3, norm2_g, final_g, weights, *, latent, tm):
    bx, lx, _ = x.shape
    row = (lambda b: b) if latent else (lambda b: CTX_ROW)

    def tok(width, col):
        return pl.BlockSpec((1, tm, width), lambda b, m: (b, m, col))

    def mod_spec(k):
        return pl.BlockSpec((1, 1, D_MODEL), lambda b, m: (row(b), 0, k))

    def resident(shape):
        return pl.BlockSpec(shape, lambda b, m: (0, 0), pipeline_mode=pl.Buffered(1))

    vec = pl.BlockSpec((1, D_MODEL), lambda b, m: (0, 0))
    in_specs = [tok(D_MODEL, 0), tok(D_MODEL, 0), tok(D_MODEL, 0), tok(SEG, 6), tok(SEG, 7),
                mod_spec(2), mod_spec(3), mod_spec(4), mod_spec(5), vec, vec]
    in_specs += [resident(w.shape) for w in weights]
    return pl.pallas_call(
        _post_kernel,
        out_shape=jax.ShapeDtypeStruct((bx, lx, D_MODEL), F32),
        grid=(bx, lx // tm),
        in_specs=in_specs,
        out_specs=tok(D_MODEL, 0),
        compiler_params=_params(2),
        name="post_latent" if latent else "post_ctx",
    )(x, y_r, o_d, proj, proj, mod3, mod3, mod3, mod3, norm2_g, final_g, *weights)


def kernel(x_prompt, x_sample, state_ret_fwd, state_ret_bwd, cache_diff_k, cache_diff_v, c, c_ctx,
           norm1_g, norm2_g, w_ada, b_ada, w_in, b_gate, ret_decay_fwd, ret_decay_bwd, ret_gn_g,
           w_ret_out, diff_lambda_q1, diff_lambda_k1, diff_lambda_q2, diff_lambda_k2, diff_subln_g,
           w_diff_out, w_o, w_ffn_gate, w_ffn_up, w_ffn_down, final_norm_g):
    batch, seq, _ = x_prompt.shape
    dec_batch, dec_seq, _ = x_sample.shape
    past_len = cache_diff_k.shape[2]
    layer = 0

    cvec = jnp.zeros((MOD_ROWS, D_MODEL), F32).at[:dec_batch].set(c).at[CTX_ROW].set(c_ctx)
    mod = _modulation(cvec, w_ada[layer], b_ada[layer][None, :])
    mod3 = mod.reshape(MOD_ROWS, 1, 6 * D_MODEL)

    w_in_bf = w_in[layer].astype(BF)
    post_w = tuple(w[layer].astype(BF) for w in (w_ret_out, w_diff_out, w_o, w_ffn_gate, w_ffn_up, w_ffn_down))
    n1 = norm1_g[layer][None, :]
    n2 = norm2_g[layer][None, :]
    fn = final_norm_g[None, :]
    bg = b_gate[layer][None, :]
    dec_f = ret_decay_fwd[layer][None, :]
    dec_b = ret_decay_bwd[layer][None, :]
    gn = ret_gn_g[layer][None, :]
    lam_params = tuple(p[layer][None, :] for p in (diff_lambda_q1, diff_lambda_k1, diff_lambda_q2, diff_lambda_k2))
    subln = diff_subln_g[layer][None, :]

    xp_flat = x_prompt.reshape(1, batch * seq, D_MODEL)
    assert seq == PROJ_ROWS
    proj_c, kt_new, v_new = _inproj(xp_flat, mod3, n1, w_in_bf, bg, None, latent=False, tm=512)
    proj_c_b = proj_c.reshape(batch, seq, W_IN_COLS)
    yr_c, s_f, s_b = _retention(proj_c_b, dec_f, dec_b, gn, None, None, emit_state=True, hb=RET_HEADS)
    od_c = _attention(proj_c_b, None, None, lam_params, subln, bq=seq, hb=DIFF_HEADS)
    y_prompt = _post(xp_flat, yr_c.reshape(1, batch * seq, -1), od_c.reshape(1, batch * seq, -1),
                     proj_c, mod3, n2, fn, post_w, latent=False, tm=512).reshape(batch, seq, D_MODEL)

    rope = _rope_tables(dec_seq)
    proj_l = _inproj(x_sample, mod3, n1, w_in_bf, bg, rope, latent=True, tm=512)
    yr_l = _retention(proj_l, dec_f, dec_b, gn, state_ret_fwd, state_ret_bwd, emit_state=False, hb=2)
    kt_ctx = jnp.transpose(cache_diff_k[:, layer], (0, 2, 3, 4, 1)).reshape(dec_batch, -1, past_len)
    v_ctx = cache_diff_v[:, layer].reshape(dec_batch, past_len, DIFF_HEADS * 2 * DIFF_DH)
    od_l = _attention(proj_l, kt_ctx, v_ctx, lam_params, subln, bq=256, hb=1)
    y_sample = _post(x_sample, yr_l, od_l, proj_l, mod3, n2, fn, post_w, latent=True, tm=512)

    new_diff_k = jnp.transpose(kt_new.reshape(batch, DIFF_HEADS, 2, DIFF_DH, seq), (0, 4, 1, 2, 3))[:, None]
    new_diff_v = v_new.reshape(batch, 1, seq, DIFF_HEADS, 2 * DIFF_DH)
    return (y_prompt, y_sample, s_f, s_b, new_diff_k, new_diff_v)
```

```python
import functools
import math

import jax
import jax.numpy as jnp
import numpy as np
from jax import lax
from jax.experimental import pallas as pl
from jax.experimental.pallas import tpu as pltpu

D_MODEL = 1024
GRID_W = 64
ROPE_BASE = 10000.0
RET_HEADS = 4
RET_DK = 128
RET_DV = 256
RET_CHUNK = 128
DIFF_HEADS = 8
DIFF_DH = 64
FFN_HIDDEN = 2816
W_IN_COLS = 8192
RMS_EPS = 1e-6
GN_EPS = 1e-5
LAM_INIT = 0.8 - 0.6 * math.exp(-0.3 * 0)

SEG = 1024
N_SEG = W_IN_COLS // SEG
LANES = 128
MOD_ROWS = 8
CTX_ROW = 4
K_R_SCALE = RET_DK ** -0.5
Q_D_SCALE = DIFF_DH ** -0.5 * math.log2(math.e)
FFN_SPLITS = ((0, 1536), (1536, FFN_HIDDEN))
PROJ_ROWS = 256
POST_ROWS = 256
ATT_KC = 256
ATT_VROWS = LANES + 16

BF = jnp.bfloat16
F32 = jnp.float32
VMEM_LIMIT = 56 * 1024 * 1024


def _params(n_axes):
    return pltpu.CompilerParams(dimension_semantics=("arbitrary",) * n_axes,
                                vmem_limit_bytes=VMEM_LIMIT)


def _mod_kernel(c_ref, w_ref, b_ref, o_ref):
    c = c_ref[...]
    s = c * jax.nn.sigmoid(c)
    o_ref[...] = jnp.dot(s, w_ref[...], preferred_element_type=F32,
                         precision=lax.Precision.HIGHEST) + b_ref[...]


def _modulation(cvec, w_ada, b_ada):
    n_out = w_ada.shape[1]
    tn = 2048
    return pl.pallas_call(
        _mod_kernel,
        out_shape=jax.ShapeDtypeStruct((MOD_ROWS, n_out), F32),
        grid=(n_out // tn,),
        in_specs=[pl.BlockSpec((MOD_ROWS, D_MODEL), lambda n: (0, 0)),
                  pl.BlockSpec((D_MODEL, tn), lambda n: (0, n)),
                  pl.BlockSpec((1, tn), lambda n: (0, n))],
        out_specs=pl.BlockSpec((MOD_ROWS, tn), lambda n: (0, n)),
        compiler_params=_params(1),
        name="modulation",
    )(cvec, w_ada, b_ada)


def _rope_tables(seq_len):
    rows = seq_len // GRID_W
    pos_r = np.repeat(np.arange(rows), GRID_W).astype(np.float64)
    pos_c = np.tile(np.arange(GRID_W), rows).astype(np.float64)

    def angles(d):
        n = d // 4
        inv = ROPE_BASE ** (-np.arange(n, dtype=np.float64) / n)
        return np.concatenate([pos_r[:, None] * inv, pos_c[:, None] * inv], axis=-1)

    ang_r = angles(RET_DK)
    cos_r = np.concatenate([np.cos(ang_r)] * 2, axis=-1)
    sin_r = np.concatenate([-np.sin(ang_r), np.sin(ang_r)], axis=-1)
    ang_d = angles(DIFF_DH)
    zero = np.zeros_like(ang_d)
    cos_d = np.concatenate([np.cos(ang_d)] * 4, axis=-1)
    sin_lo = np.concatenate([zero, np.sin(ang_d)] * 2, axis=-1)
    sin_hi = np.concatenate([-np.sin(ang_d), zero] * 2, axis=-1)
    return tuple(jnp.asarray(t, dtype=F32) for t in (cos_r, sin_r, cos_d, sin_lo, sin_hi))


def _inproj_kernel(*refs, latent, tm):
    if latent:
        (x_ref, sh_ref, sc_ref, g1_ref, w_ref, bg_ref,
         cr_ref, sr_ref, cd_ref, slo_ref, shi_ref, o_ref) = refs
    else:
        x_ref, sh_ref, sc_ref, g1_ref, w_ref, bg_ref, o_ref, ko_ref, vo_ref = refs
    heads = SEG // LANES

    def seg(n):
        return slice(n * SEG, (n + 1) * SEG)

    def normed(rs):
        x = x_ref[0, rs, :]
        ms = jnp.mean(x * x, axis=-1, keepdims=True)
        y = x * lax.rsqrt(ms + RMS_EPS) * g1_ref[...]
        return (y * (1.0 + sc_ref[0]) + sh_ref[0]).astype(BF)

    def rope_r(a, rs):
        return a * cr_ref[rs, :] + pltpu.roll(a, 64, 1) * sr_ref[rs, :]

    def rope_d(a, rs):
        return (a * cd_ref[rs, :] + pltpu.roll(a, 32, 1) * slo_ref[rs, :]
                + pltpu.roll(a, 96, 1) * shi_ref[rs, :])

    def per_head(rs, n, acc, fn):
        for j in range(heads):
            lo = n * SEG + j * LANES
            o_ref[0, rs, lo:lo + LANES] = fn(acc[:, j * LANES:(j + 1) * LANES], j).astype(BF)

    for r in range(tm // PROJ_ROWS):
        rs = slice(r * PROJ_ROWS, (r + 1) * PROJ_ROWS)
        h = normed(rs)

        def project(n):
            return jnp.dot(h, w_ref[:, seg(n)], preferred_element_type=F32)

        def q_or_k_r(a, j):
            a = rope_r(a, rs) if latent else a
            return a * K_R_SCALE if j >= heads // 2 else a

        per_head(rs, 0, project(0), q_or_k_r)
        o_ref[0, rs, seg(1)] = project(1).astype(BF)
        acc = project(2)
        o_ref[0, rs, seg(2)] = (acc * jax.nn.sigmoid(acc)).astype(BF)
        acc = project(3)
        if latent:
            per_head(rs, 3, acc, lambda a, j: rope_d(a, rs) * Q_D_SCALE)
        else:
            o_ref[0, rs, seg(3)] = (acc * Q_D_SCALE).astype(BF)
        acc = project(4)
        if latent:
            per_head(rs, 4, acc, lambda a, j: rope_d(a, rs))
        else:
            o_ref[0, rs, seg(4)] = acc.astype(BF)
            ko_ref[r] = acc.T
        acc = project(5)
        o_ref[0, rs, seg(5)] = acc.astype(BF)
        if not latent:
            vo_ref[0, rs, :] = acc
        for n in (6, 7):
            gate_cols = slice((n - 6) * SEG, (n - 5) * SEG)
            o_ref[0, rs, seg(n)] = jax.nn.sigmoid(project(n) + bg_ref[:, gate_cols]).astype(BF)


def _inproj(x, mod3, norm_g, w_in_bf, b_gate, rope, *, latent, tm):
    bx, lx, _ = x.shape
    row = (lambda b: b) if latent else (lambda b: CTX_ROW)

    def mod_spec(k):
        return pl.BlockSpec((1, 1, D_MODEL), lambda b, m: (row(b), 0, k))

    def resident(shape):
        return pl.BlockSpec(shape, lambda b, m: (0, 0), pipeline_mode=pl.Buffered(1))

    in_specs = [pl.BlockSpec((1, tm, D_MODEL), lambda b, m: (b, m, 0)),
                mod_spec(0), mod_spec(1),
                resident(norm_g.shape), resident(w_in_bf.shape), resident(b_gate.shape)]
    args = [x, mod3, mod3, norm_g, w_in_bf, b_gate]
    proj_shape = jax.ShapeDtypeStruct((bx, lx, W_IN_COLS), BF)
    proj_spec = pl.BlockSpec((1, tm, W_IN_COLS), lambda b, m: (b, m, 0))
    if latent:
        in_specs += [pl.BlockSpec((tm, LANES), lambda b, m: (m, 0))] * 5
        args += list(rope)
        out_shape, out_specs = proj_shape, proj_spec
    else:
        v_shape = jax.ShapeDtypeStruct((bx, lx, SEG), F32)
        v_spec = pl.BlockSpec((1, tm, SEG), lambda b, m: (b, m, 0))
        kt_shape = jax.ShapeDtypeStruct((bx * lx // PROJ_ROWS, SEG, PROJ_ROWS), F32)
        kt_spec = pl.BlockSpec((tm // PROJ_ROWS, SEG, PROJ_ROWS), lambda b, m: (b * (lx // tm) + m, 0, 0))
        out_shape, out_specs = (proj_shape, kt_shape, v_shape), (proj_spec, kt_spec, v_spec)
    return pl.pallas_call(
        functools.partial(_inproj_kernel, latent=latent, tm=tm),
        out_shape=out_shape,
        grid=(bx, lx // tm),
        in_specs=in_specs,
        out_specs=out_specs,
        compiler_params=_params(2),
        name="inproj_latent" if latent else "inproj_ctx",
    )(*args)


def _ret_unroll(hb, n_chunks):
    return max(1, min(8 // hb, n_chunks))


def _ret_kernel(*refs, n_chunks, hb, has_s0, emit_state):
    it = iter(refs)
    q_ref, k_ref, v_ref, g_ref, df_ref, db_ref, gn_ref = (next(it) for _ in range(7))
    s0f_ref, s0b_ref = (next(it), next(it)) if has_s0 else (None, None)
    y_ref = next(it)
    sfo_ref, sbo_ref = (next(it), next(it)) if emit_state else (None, None)
    sf_scr, sb_scr, o_scr = next(it), next(it), next(it)
    C = RET_CHUNK
    unroll = _ret_unroll(hb, n_chunks)
    lane = lax.broadcasted_iota(jnp.int32, (1, RET_HEADS), 1)
    lg_f_all = jax.nn.log_sigmoid(df_ref[...].astype(F32))
    lg_b_all = jax.nn.log_sigmoid(db_ref[...].astype(F32))
    ri = lax.broadcasted_iota(jnp.int32, (C, C), 0).astype(F32)
    ci = lax.broadcasted_iota(jnp.int32, (C, C), 1).astype(F32)
    dist = ri - ci
    rk = lax.broadcasted_iota(jnp.int32, (C, RET_DK), 0).astype(F32)

    def head_consts(j):
        head = pl.program_id(1) * hb + j
        lgf = jnp.sum(jnp.where(lane == head, lg_f_all, 0.0), axis=-1, keepdims=True)
        lgb = jnp.sum(jnp.where(lane == head, lg_b_all, 0.0), axis=-1, keepdims=True)
        decay = jnp.where(dist >= 0, jnp.exp(lgf * jnp.maximum(dist, 0.0)),
                          jnp.exp(lgb * jnp.maximum(-dist, 0.0)))
        return dict(decay=decay,
                    wq_f=jnp.exp(lgf * (rk + 1.0)), wq_b=jnp.exp(lgb * (C - rk)),
                    wk_f=jnp.exp(lgf * (C - 1.0 - rk)), wk_b=jnp.exp(lgb * rk),
                    gc_f=jnp.exp(lgf * C), gc_b=jnp.exp(lgb * C))

    consts = [head_consts(j) for j in range(hb)]

    def rows(c):
        return pl.ds(pl.multiple_of(c * C, C), C)

    def cols(j, width):
        return slice(j * width, (j + 1) * width)

    def run(streams):
        live = list(streams)
        while live:
            live = [g for g in live if next(g, StopIteration) is not StopIteration]

    def kv_chunk(j, c, wk):
        kw = (k_ref[0, rows(c), cols(j, RET_DK)].astype(F32) * wk).astype(BF)
        return lax.dot_general(kw, v_ref[0, rows(c), cols(j, RET_DV)], (((0,), (0,)), ((), ())),
                               preferred_element_type=F32)

    def scan_body(g, carry):
        state = list(carry)
        kvs = {}

        def products(i):
            n = g * unroll + i
            for j in range(hb):
                kvs[i, j] = (kv_chunk(j, n, consts[j]["wk_f"]), kv_chunk(j, n_chunks - 1 - n, consts[j]["wk_b"]))

        def updates(i):
            n = g * unroll + i
            for j in range(hb):
                sf, sb = state[2 * j], state[2 * j + 1]
                sf_scr[j, n] = sf.astype(BF)
                sb_scr[j, n_chunks - 1 - n] = sb.astype(BF)
                state[2 * j] = consts[j]["gc_f"] * sf + kvs[i, j][0]
                state[2 * j + 1] = consts[j]["gc_b"] * sb + kvs[i, j][1]

        products(0)
        for i in range(1, unroll):
            products(i)
            updates(i - 1)
        updates(unroll - 1)
        return tuple(state)

    init = []
    for j in range(hb):
        if has_s0:
            init += [s0f_ref[j].astype(F32), s0b_ref[j].astype(F32)]
        else:
            init += [jnp.zeros((RET_DK, RET_DV), F32)] * 2
    final = lax.fori_loop(0, n_chunks // unroll, scan_body, tuple(init))
    if emit_state:
        for j in range(hb):
            sfo_ref[j] = final[2 * j]
            sbo_ref[j] = final[2 * j + 1]

    def mix_chunk(j, c, buf, idx):
        k = consts[j]
        q = q_ref[0, rows(c), cols(j, RET_DK)]
        qf32 = q.astype(F32)
        v = v_ref[0, rows(c), cols(j, RET_DV)]
        sc = lax.dot_general(q, k_ref[0, rows(c), cols(j, RET_DK)], (((1,), (1,)), ((), ())),
                             preferred_element_type=F32)
        yield
        o = jnp.dot((sc * k["decay"]).astype(BF), v, preferred_element_type=F32)
        o = o + jnp.dot((qf32 * k["wq_f"]).astype(BF), sf_scr[j, c], preferred_element_type=F32)
        o = o + jnp.dot((qf32 * k["wq_b"]).astype(BF), sb_scr[j, c], preferred_element_type=F32)
        o_scr[buf, idx] = o

    def norm_chunk(j, c, buf, idx):
        o = o_scr[buf, idx]
        mu = jnp.mean(o, axis=-1, keepdims=True)
        yield
        oc = o - mu
        var = jnp.mean(oc * oc, axis=-1, keepdims=True)
        yield
        y = oc * lax.rsqrt(var + GN_EPS) * gn_ref[:, cols(j, RET_DV)]
        gate = g_ref[0, rows(c), cols(j, RET_DV)].astype(F32)
        y_ref[0, rows(c), cols(j, RET_DV)] = (gate * y).astype(BF)

    def group(fn, g, buf):
        return [fn(j, g * unroll + i, buf, i * hb + j) for i in range(unroll) for j in range(hb)]

    n_groups = n_chunks // unroll
    run(group(mix_chunk, 0, 0))
    if n_groups > 1:
        assert n_groups % 2 == 0
        run(group(mix_chunk, 1, 1) + group(norm_chunk, 0, 0))

        def out_body(u, _):
            run(group(mix_chunk, 2 * u, 0) + group(norm_chunk, 2 * u - 1, 1))
            run(group(mix_chunk, 2 * u + 1, 1) + group(norm_chunk, 2 * u, 0))
            return 0

        lax.fori_loop(1, n_groups // 2, out_body, 0)
    run(group(norm_chunk, n_groups - 1, (n_groups - 1) % 2))


def _retention(proj, decay_f, decay_b, gn_g, s0_f, s0_b, *, emit_state, hb):
    bx, lx, _ = proj.shape
    n_chunks = lx // RET_CHUNK
    has_s0 = s0_f is not None
    groups = RET_HEADS // hb
    k_off = (RET_HEADS * RET_DK) // (hb * RET_DK)
    v_off = (2 * RET_HEADS * RET_DK) // (hb * RET_DV)
    g_off = v_off + groups
    state_spec = pl.BlockSpec((None, None, hb, RET_DK, RET_DV), lambda b, h: (b, 0, h, 0, 0))
    in_specs = [pl.BlockSpec((1, lx, hb * RET_DK), lambda b, h: (b, 0, h)),
                pl.BlockSpec((1, lx, hb * RET_DK), lambda b, h: (b, 0, k_off + h)),
                pl.BlockSpec((1, lx, hb * RET_DV), lambda b, h: (b, 0, v_off + h)),
                pl.BlockSpec((1, lx, hb * RET_DV), lambda b, h: (b, 0, g_off + h)),
                pl.BlockSpec((1, RET_HEADS), lambda b, h: (0, 0)),
                pl.BlockSpec((1, RET_HEADS), lambda b, h: (0, 0)),
                pl.BlockSpec((1, hb * RET_DV), lambda b, h: (0, h))]
    args = [proj, proj, proj, proj, decay_f, decay_b, gn_g]
    if has_s0:
        in_specs += [state_spec, state_spec]
        args += [s0_f, s0_b]
    y_shape = jax.ShapeDtypeStruct((bx, lx, RET_HEADS * RET_DV), BF)
    y_spec = pl.BlockSpec((1, lx, hb * RET_DV), lambda b, h: (b, 0, h))
    if emit_state:
        st_shape = jax.ShapeDtypeStruct((bx, 1, RET_HEADS, RET_DK, RET_DV), F32)
        out_shape, out_specs = (y_shape, st_shape, st_shape), (y_spec, state_spec, state_spec)
    else:
        out_shape, out_specs = y_shape, y_spec
    snap = pltpu.VMEM((hb, n_chunks, RET_DK, RET_DV), BF)
    return pl.pallas_call(
        functools.partial(_ret_kernel, n_chunks=n_chunks, hb=hb, has_s0=has_s0, emit_state=emit_state),
        out_shape=out_shape,
        grid=(bx, groups),
        in_specs=in_specs,
        out_specs=out_specs,
        scratch_shapes=[snap, snap,
                        pltpu.VMEM((2, _ret_unroll(hb, n_chunks) * hb, RET_CHUNK, RET_DV), F32)],
        compiler_params=_params(2),
        name="retention_latent" if has_s0 else "retention_ctx",
    )(*args)


def _attn_kernel(*refs, lx, lk_ctx, bq, hb):
    it = iter(refs)
    q_ref, k_ref, v_ref = next(it), next(it), next(it)
    kctx_ref, vctx_ref = (next(it), next(it)) if lk_ctx else (None, None)
    lq1_ref, lk1_ref, lq2_ref, lk2_ref, sg_ref = (next(it) for _ in range(5))
    o_ref = next(it)
    vt, s_scr, m_scr, acc_scr = next(it), next(it), next(it), next(it)
    kctx_scr = next(it) if lk_ctx else None
    lk = lx + lk_ctx
    n_kc = lk // ATT_KC
    n_q = lx // bq

    def cols(j):
        return slice(j * LANES, (j + 1) * LANES)

    def chunk(c):
        return slice(c * ATT_KC, (c + 1) * ATT_KC)

    def keys(j, c):
        lo = c * ATT_KC
        if lo < lx:
            return k_ref[0, lo:lo + ATT_KC, cols(j)]
        return kctx_scr[j, lo - lx:lo - lx + ATT_KC, :]

    for j in range(hb):
        if lk_ctx:
            kctx_scr[j] = kctx_ref[0, cols(j), :].T.astype(BF)
        for c in range(n_kc):
            lo = c * ATT_KC
            if lo < lx:
                vc = v_ref[0, lo:lo + ATT_KC, cols(j)].astype(F32)
            else:
                vc = vctx_ref[0, lo - lx:lo - lx + ATT_KC, cols(j)]
            vt[j, 0:LANES, chunk(c)] = vc.T.astype(BF)
        vt[j, LANES:ATT_VROWS, :] = jnp.ones((ATT_VROWS - LANES, lk), BF)

    def lam_term(a_ref, b_ref):
        return jnp.exp(jnp.sum(a_ref[...].astype(F32) * b_ref[...].astype(F32), axis=-1, keepdims=True))

    lam = lam_term(lq1_ref, lk1_ref) - lam_term(lq2_ref, lk2_ref) + LAM_INIT
    lane = lax.broadcasted_iota(jnp.int32, (bq, LANES), 1)

    def rows(t):
        return pl.ds(pl.multiple_of(t * bq, bq), bq)

    def scores(j, t, slot):
        q = q_ref[0, rows(t), cols(j)]
        zero = jnp.zeros_like(q)
        q_maps = (jnp.where(lane < DIFF_DH, q, zero), jnp.where(lane >= DIFF_DH, q, zero))
        m = [jnp.full((8, bq), -jnp.inf, F32) for _ in range(2)]
        for c in range(n_kc):
            kk = keys(j, c)
            for i in range(2):
                s = lax.dot_general(kk, q_maps[i], (((1,), (1,)), ((), ())),
                                    preferred_element_type=F32)
                s_scr[j, slot, i, chunk(c), :] = s
                m[i] = jnp.maximum(m[i], jnp.max(s.reshape(ATT_KC // 8, 8, bq), axis=0))
            yield
        for i in range(2):
            m_scr[j, slot, i] = jnp.broadcast_to(jnp.max(m[i], axis=0, keepdims=True), (8, bq))

    def values(j, slot):
        m = [m_scr[j, slot, i][0:1, :] for i in range(2)]
        acc = [jnp.zeros((ATT_VROWS, bq), F32) for _ in range(2)]
        for c in range(n_kc):
            for i in range(2):
                e = jnp.exp2(s_scr[j, slot, i, chunk(c), :] - m[i]).astype(BF)
                acc[i] = acc[i] + jnp.dot(vt[j, :, chunk(c)], e, preferred_element_type=F32)
            yield
        for i in range(2):
            acc_scr[j, slot, i] = acc[i]

    def finish(j, t, slot):
        outs = [acc_scr[j, slot, i, 0:LANES, :] / acc_scr[j, slot, i, LANES:LANES + 1, :] for i in range(2)]
        ot = outs[0] - lam * outs[1]
        ms = jnp.mean(ot * ot, axis=0, keepdims=True)
        o = (ot * lax.rsqrt(ms + RMS_EPS)).T * sg_ref[...]
        o_ref[0, rows(t), cols(j)] = (o * (1.0 - LAM_INIT)).astype(BF)
        yield

    def chain(*streams):
        for g in streams:
            yield from g

    def run(*streams):
        live = list(streams)
        while live:
            live = [g for g in live if next(g, StopIteration) is not StopIteration]

    if n_q == 1:
        run(*[chain(scores(j, 0, 0), values(j, 0), finish(j, 0, 0)) for j in range(hb)])
        return
    assert hb == 1 and n_q % 2 == 0 and n_q >= 4
    run(scores(0, 0, 0))
    run(scores(0, 1, 1), values(0, 0))
    run(scores(0, 2, 0), chain(finish(0, 0, 0), values(0, 1)))

    def body(u, _):
        t = 2 * u
        run(scores(0, t + 1, 1), chain(finish(0, t - 1, 1), values(0, 0)))
        run(scores(0, t + 2, 0), chain(finish(0, t, 0), values(0, 1)))
        return 0

    lax.fori_loop(1, (n_q - 2) // 2, body, 0)
    run(scores(0, n_q - 1, 1), chain(finish(0, n_q - 3, 1), values(0, 0)))
    run(chain(finish(0, n_q - 2, 0), values(0, 1)))
    run(finish(0, n_q - 1, 1))


def _attention(proj, kt_ctx, v_ctx, lam_params, subln_g, *, bq, hb):
    bx, lx, _ = proj.shape
    lk_ctx = 0 if kt_ctx is None else kt_ctx.shape[2]
    lk = lx + lk_ctx
    head_w = 2 * DIFF_DH
    width = hb * head_w
    q_off = 3 * SEG // width
    k_off = 4 * SEG // width
    v_off = 5 * SEG // width

    def heads(off):
        return pl.BlockSpec((1, lx, width), lambda b, h: (b, 0, off + h))

    in_specs = [heads(q_off), heads(k_off), heads(v_off)]
    args = [proj, proj, proj]
    scratch = [pltpu.VMEM((hb, ATT_VROWS, lk), BF),
               pltpu.VMEM((hb, 2, 2, lk, bq), F32), pltpu.VMEM((hb, 2, 2, 8, bq), F32),
               pltpu.VMEM((hb, 2, 2, ATT_VROWS, bq), F32)]
    if lk_ctx:
        assert lk_ctx % ATT_KC == 0
        in_specs += [pl.BlockSpec((1, width, lk_ctx), lambda b, h: (b, h, 0)),
                     pl.BlockSpec((1, lk_ctx, width), lambda b, h: (b, 0, h))]
        args += [kt_ctx, v_ctx]
        scratch.append(pltpu.VMEM((hb, lk_ctx, head_w), BF))
    small = pl.BlockSpec((1, DIFF_DH), lambda b, h: (0, 0))
    in_specs += [small] * 4 + [pl.BlockSpec((1, head_w), lambda b, h: (0, 0))]
    args += list(lam_params) + [subln_g]
    return pl.pallas_call(
        functools.partial(_attn_kernel, lx=lx, lk_ctx=lk_ctx, bq=bq, hb=hb),
        out_shape=jax.ShapeDtypeStruct((bx, lx, DIFF_HEADS * head_w), BF),
        grid=(bx, DIFF_HEADS // hb),
        in_specs=in_specs,
        out_specs=heads(0),
        scratch_shapes=scratch,
        compiler_params=_params(2),
        name="diff_attn_latent" if lk_ctx else "diff_attn_ctx",
    )(*args)


def _post_kernel(x_ref, yr_ref, od_ref, gr_ref, gd_ref, gate1_ref, sh2_ref, sc2_ref, gate2_ref,
                 n2_ref, fn_ref, wr_ref, wd_ref, wo_ref, wg_ref, wu_ref, wdn_ref, o_ref, *, tm):
    def rows_stream(rs):
        br = jnp.dot(yr_ref[0, rs, :], wr_ref[...], preferred_element_type=F32)
        bd = jnp.dot(od_ref[0, rs, :], wd_ref[...], preferred_element_type=F32)
        yield
        merged = (gr_ref[0, rs, :].astype(F32) * br + gd_ref[0, rs, :].astype(F32) * bd).astype(BF)
        mix = jnp.dot(merged, wo_ref[...], preferred_element_type=F32)
        yield
        x1 = x_ref[0, rs, :] + gate1_ref[0] * mix
        ms = jnp.mean(x1 * x1, axis=-1, keepdims=True)
        h2 = x1 * lax.rsqrt(ms + RMS_EPS) * n2_ref[...]
        h2 = (h2 * (1.0 + sc2_ref[0]) + sh2_ref[0]).astype(BF)
        ffn = None
        for lo, hi in FFN_SPLITS:
            g = jnp.dot(h2, wg_ref[:, lo:hi], preferred_element_type=F32)
            u = jnp.dot(h2, wu_ref[:, lo:hi], preferred_element_type=F32)
            yield
            act = (g * jax.nn.sigmoid(g) * u).astype(BF)
            part = jnp.dot(act, wdn_ref[lo:hi, :], preferred_element_type=F32)
            ffn = part if ffn is None else ffn + part
            yield
        x2 = x1 + gate2_ref[0] * ffn
        ms2 = jnp.mean(x2 * x2, axis=-1, keepdims=True)
        o_ref[0, rs, :] = x2 * lax.rsqrt(ms2 + RMS_EPS) * fn_ref[...]

    def delayed(stream, stages):
        for _ in range(stages):
            yield
        yield from stream

    n_groups = tm // POST_ROWS
    live = [delayed(rows_stream(slice(r * POST_ROWS, (r + 1) * POST_ROWS)), r) for r in range(n_groups)]
    while live:
        live = [g for g in live if next(g, StopIteration) is not StopIteration]


def _post(x, y_r, o_d, proj, mod3, norm2_g, final_g, weights, *, latent, tm):
    bx, lx, _ = x.shape
    row = (lambda b: b) if latent else (lambda b: CTX_ROW)

    def tok(width, col):
        return pl.BlockSpec((1, tm, width), lambda b, m: (b, m, col))

    def mod_spec(k):
        return pl.BlockSpec((1, 1, D_MODEL), lambda b, m: (row(b), 0, k))

    def resident(shape):
        return pl.BlockSpec(shape, lambda b, m: (0, 0), pipeline_mode=pl.Buffered(1))

    vec = pl.BlockSpec((1, D_MODEL), lambda b, m: (0, 0))
    in_specs = [tok(D_MODEL, 0), tok(D_MODEL, 0), tok(D_MODEL, 0), tok(SEG, 6), tok(SEG, 7),
                mod_spec(2), mod_spec(3), mod_spec(4), mod_spec(5), vec, vec]
    in_specs += [resident(w.shape) for w in weights]
    return pl.pallas_call(
        functools.partial(_post_kernel, tm=tm),
        out_shape=jax.ShapeDtypeStruct((bx, lx, D_MODEL), F32),
        grid=(bx, lx // tm),
        in_specs=in_specs,
        out_specs=tok(D_MODEL, 0),
        compiler_params=_params(2),
        name="post_latent" if latent else "post_ctx",
    )(x, y_r, o_d, proj, proj, mod3, mod3, mod3, mod3, norm2_g, final_g, *weights)


def kernel(x_prompt, x_sample, state_ret_fwd, state_ret_bwd, cache_diff_k, cache_diff_v, c, c_ctx,
           norm1_g, norm2_g, w_ada, b_ada, w_in, b_gate, ret_decay_fwd, ret_decay_bwd, ret_gn_g,
           w_ret_out, diff_lambda_q1, diff_lambda_k1, diff_lambda_q2, diff_lambda_k2, diff_subln_g,
           w_diff_out, w_o, w_ffn_gate, w_ffn_up, w_ffn_down, final_norm_g):
    batch, seq, _ = x_prompt.shape
    dec_batch, dec_seq, _ = x_sample.shape
    past_len = cache_diff_k.shape[2]
    layer = 0

    cvec = jnp.zeros((MOD_ROWS, D_MODEL), F32).at[:dec_batch].set(c).at[CTX_ROW].set(c_ctx)
    mod = _modulation(cvec, w_ada[layer], b_ada[layer][None, :])
    mod3 = mod.reshape(MOD_ROWS, 1, 6 * D_MODEL)

    w_in_bf = w_in[layer].astype(BF)
    post_w = tuple(w[layer].astype(BF) for w in (w_ret_out, w_diff_out, w_o, w_ffn_gate, w_ffn_up, w_ffn_down))
    n1 = norm1_g[layer][None, :]
    n2 = norm2_g[layer][None, :]
    fn = final_norm_g[None, :]
    bg = b_gate[layer][None, :]
    dec_f = ret_decay_fwd[layer][None, :]
    dec_b = ret_decay_bwd[layer][None, :]
    gn = ret_gn_g[layer][None, :]
    lam_params = tuple(p[layer][None, :] for p in (diff_lambda_q1, diff_lambda_k1, diff_lambda_q2, diff_lambda_k2))
    subln = diff_subln_g[layer][None, :]

    xp_flat = x_prompt.reshape(1, batch * seq, D_MODEL)
    assert seq == PROJ_ROWS
    proj_c, kt_new, v_new = _inproj(xp_flat, mod3, n1, w_in_bf, bg, None, latent=False, tm=512)
    proj_c_b = proj_c.reshape(batch, seq, W_IN_COLS)
    yr_c, s_f, s_b = _retention(proj_c_b, dec_f, dec_b, gn, None, None, emit_state=True, hb=RET_HEADS)
    od_c = _attention(proj_c_b, None, None, lam_params, subln, bq=seq, hb=DIFF_HEADS)
    y_prompt = _post(xp_flat, yr_c.reshape(1, batch * seq, -1), od_c.reshape(1, batch * seq, -1),
                     proj_c, mod3, n2, fn, post_w, latent=False, tm=512).reshape(batch, seq, D_MODEL)

    rope = _rope_tables(dec_seq)
    proj_l = _inproj(x_sample, mod3, n1, w_in_bf, bg, rope, latent=True, tm=512)
    yr_l = _retention(proj_l, dec_f, dec_b, gn, state_ret_fwd, state_ret_bwd, emit_state=False, hb=2)
    kt_ctx = jnp.transpose(cache_diff_k[:, layer], (0, 2, 3, 4, 1)).reshape(dec_batch, -1, past_len)
    v_ctx = cache_diff_v[:, layer].reshape(dec_batch, past_len, DIFF_HEADS * 2 * DIFF_DH)
    od_l = _attention(proj_l, kt_ctx, v_ctx, lam_params, subln, bq=256, hb=1)
    y_sample = _post(x_sample, yr_l, od_l, proj_l, mod3, n2, fn, post_w, latent=True, tm=512)

    new_diff_k = jnp.transpose(kt_new.reshape(batch, DIFF_HEADS, 2, DIFF_DH, seq), (0, 4, 1, 2, 3))[:, None]
    new_diff_v = v_new.reshape(batch, 1, seq, DIFF_HEADS, 2 * DIFF_DH)
    return (y_prompt, y_sample, s_f, s_b, new_diff_k, new_diff_v)
```

```python
import functools
import math

import jax
import jax.numpy as jnp
import numpy as np
from jax import lax
from jax.experimental import pallas as pl
from jax.experimental.pallas import tpu as pltpu

D_MODEL = 1024
GRID_W = 64
ROPE_BASE = 10000.0
RET_HEADS = 4
RET_DK = 128
RET_DV = 256
RET_CHUNK = 128
DIFF_HEADS = 8
DIFF_DH = 64
FFN_HIDDEN = 2816
W_IN_COLS = 8192
RMS_EPS = 1e-6
GN_EPS = 1e-5
LAM_INIT = 0.8 - 0.6 * math.exp(-0.3 * 0)

SEG = 1024
N_SEG = W_IN_COLS // SEG
LANES = 128
MOD_ROWS = 8
CTX_ROW = 4
K_R_SCALE = RET_DK ** -0.5
Q_D_SCALE = DIFF_DH ** -0.5 * math.log2(math.e)
FFN_SPLITS = ((0, 1536), (1536, FFN_HIDDEN))
PROJ_ROWS = 256
POST_ROWS = 256
ATT_KC = 256
ATT_VROWS = LANES + 16

BF = jnp.bfloat16
F32 = jnp.float32
VMEM_LIMIT = 56 * 1024 * 1024


def _params(n_axes):
    return pltpu.CompilerParams(dimension_semantics=("arbitrary",) * n_axes,
                                vmem_limit_bytes=VMEM_LIMIT)


def _mod_kernel(c_ref, w_ref, b_ref, o_ref):
    c = c_ref[...]
    s = c * jax.nn.sigmoid(c)
    o_ref[...] = jnp.dot(s, w_ref[...], preferred_element_type=F32) + b_ref[...]


def _modulation(cvec, w_ada, b_ada):
    n_out = w_ada.shape[1]
    tn = 2048
    return pl.pallas_call(
        _mod_kernel,
        out_shape=jax.ShapeDtypeStruct((MOD_ROWS, n_out), F32),
        grid=(n_out // tn,),
        in_specs=[pl.BlockSpec((MOD_ROWS, D_MODEL), lambda n: (0, 0)),
                  pl.BlockSpec((D_MODEL, tn), lambda n: (0, n)),
                  pl.BlockSpec((1, tn), lambda n: (0, n))],
        out_specs=pl.BlockSpec((MOD_ROWS, tn), lambda n: (0, n)),
        compiler_params=_params(1),
        name="modulation",
    )(cvec, w_ada, b_ada)


def _rope_tables(seq_len):
    rows = seq_len // GRID_W
    pos_r = np.repeat(np.arange(rows), GRID_W).astype(np.float64)
    pos_c = np.tile(np.arange(GRID_W), rows).astype(np.float64)

    def angles(d):
        n = d // 4
        inv = ROPE_BASE ** (-np.arange(n, dtype=np.float64) / n)
        return np.concatenate([pos_r[:, None] * inv, pos_c[:, None] * inv], axis=-1)

    ang_r = angles(RET_DK)
    cos_r = np.concatenate([np.cos(ang_r)] * 2, axis=-1)
    sin_r = np.concatenate([-np.sin(ang_r), np.sin(ang_r)], axis=-1)
    ang_d = angles(DIFF_DH)
    zero = np.zeros_like(ang_d)
    cos_d = np.concatenate([np.cos(ang_d)] * 4, axis=-1)
    sin_lo = np.concatenate([zero, np.sin(ang_d)] * 2, axis=-1)
    sin_hi = np.concatenate([-np.sin(ang_d), zero] * 2, axis=-1)
    return tuple(jnp.asarray(t, dtype=F32) for t in (cos_r, sin_r, cos_d, sin_lo, sin_hi))


def _inproj_kernel(*refs, latent, tm):
    if latent:
        (x_ref, sh_ref, sc_ref, g1_ref, w_ref, bg_ref,
         cr_ref, sr_ref, cd_ref, slo_ref, shi_ref, o_ref) = refs
    else:
        x_ref, sh_ref, sc_ref, g1_ref, w_ref, bg_ref, o_ref, ko_ref, vo_ref = refs
    heads = SEG // LANES

    def seg(n):
        return slice(n * SEG, (n + 1) * SEG)

    def normed(rs):
        x = x_ref[0, rs, :]
        ms = jnp.mean(x * x, axis=-1, keepdims=True)
        y = x * lax.rsqrt(ms + RMS_EPS) * g1_ref[...]
        return (y * (1.0 + sc_ref[0]) + sh_ref[0]).astype(BF)

    def rope_r(a, rs):
        return a * cr_ref[rs, :] + pltpu.roll(a, 64, 1) * sr_ref[rs, :]

    def rope_d(a, rs):
        return (a * cd_ref[rs, :] + pltpu.roll(a, 32, 1) * slo_ref[rs, :]
                + pltpu.roll(a, 96, 1) * shi_ref[rs, :])

    def per_head(rs, n, acc, fn):
        for j in range(heads):
            lo = n * SEG + j * LANES
            o_ref[0, rs, lo:lo + LANES] = fn(acc[:, j * LANES:(j + 1) * LANES], j).astype(BF)

    for r in range(tm // PROJ_ROWS):
        rs = slice(r * PROJ_ROWS, (r + 1) * PROJ_ROWS)
        h = normed(rs)

        def project(n):
            return jnp.dot(h, w_ref[:, seg(n)], preferred_element_type=F32)

        def q_or_k_r(a, j):
            a = rope_r(a, rs) if latent else a
            return a * K_R_SCALE if j >= heads // 2 else a

        per_head(rs, 0, project(0), q_or_k_r)
        o_ref[0, rs, seg(1)] = project(1).astype(BF)
        acc = project(2)
        o_ref[0, rs, seg(2)] = (acc * jax.nn.sigmoid(acc)).astype(BF)
        acc = project(3)
        if latent:
            per_head(rs, 3, acc, lambda a, j: rope_d(a, rs) * Q_D_SCALE)
        else:
            o_ref[0, rs, seg(3)] = (acc * Q_D_SCALE).astype(BF)
        acc = project(4)
        if latent:
            per_head(rs, 4, acc, lambda a, j: rope_d(a, rs))
        else:
            o_ref[0, rs, seg(4)] = acc.astype(BF)
            ko_ref[r] = acc.T
        acc = project(5)
        o_ref[0, rs, seg(5)] = acc.astype(BF)
        if not latent:
            vo_ref[0, rs, :] = acc
        for n in (6, 7):
            gate_cols = slice((n - 6) * SEG, (n - 5) * SEG)
            o_ref[0, rs, seg(n)] = jax.nn.sigmoid(project(n) + bg_ref[:, gate_cols]).astype(BF)


def _inproj(x, mod3, norm_g, w_in_bf, b_gate, rope, *, latent, tm):
    bx, lx, _ = x.shape
    row = (lambda b: b) if latent else (lambda b: CTX_ROW)

    def mod_spec(k):
        return pl.BlockSpec((1, 1, D_MODEL), lambda b, m: (row(b), 0, k))

    def resident(shape):
        return pl.BlockSpec(shape, lambda b, m: (0, 0), pipeline_mode=pl.Buffered(1))

    in_specs = [pl.BlockSpec((1, tm, D_MODEL), lambda b, m: (b, m, 0)),
                mod_spec(0), mod_spec(1),
                resident(norm_g.shape), resident(w_in_bf.shape), resident(b_gate.shape)]
    args = [x, mod3, mod3, norm_g, w_in_bf, b_gate]
    proj_shape = jax.ShapeDtypeStruct((bx, lx, W_IN_COLS), BF)
    proj_spec = pl.BlockSpec((1, tm, W_IN_COLS), lambda b, m: (b, m, 0))
    if latent:
        in_specs += [pl.BlockSpec((tm, LANES), lambda b, m: (m, 0))] * 5
        args += list(rope)
        out_shape, out_specs = proj_shape, proj_spec
    else:
        v_shape = jax.ShapeDtypeStruct((bx, lx, SEG), F32)
        v_spec = pl.BlockSpec((1, tm, SEG), lambda b, m: (b, m, 0))
        kt_shape = jax.ShapeDtypeStruct((bx * lx // PROJ_ROWS, SEG, PROJ_ROWS), F32)
        kt_spec = pl.BlockSpec((tm // PROJ_ROWS, SEG, PROJ_ROWS), lambda b, m: (b * (lx // tm) + m, 0, 0))
        out_shape, out_specs = (proj_shape, kt_shape, v_shape), (proj_spec, kt_spec, v_spec)
    return pl.pallas_call(
        functools.partial(_inproj_kernel, latent=latent, tm=tm),
        out_shape=out_shape,
        grid=(bx, lx // tm),
        in_specs=in_specs,
        out_specs=out_specs,
        compiler_params=_params(2),
        name="inproj_latent" if latent else "inproj_ctx",
    )(*args)


def _ret_unroll(hb, n_chunks):
    return max(1, min(8 // hb, n_chunks))


def _ret_kernel(*refs, n_chunks, hb, has_s0, emit_state):
    it = iter(refs)
    q_ref, k_ref, v_ref, g_ref, df_ref, db_ref, gn_ref = (next(it) for _ in range(7))
    s0f_ref, s0b_ref = (next(it), next(it)) if has_s0 else (None, None)
    y_ref = next(it)
    sfo_ref, sbo_ref = (next(it), next(it)) if emit_state else (None, None)
    sf_scr, sb_scr, o_scr = next(it), next(it), next(it)
    C = RET_CHUNK
    unroll = _ret_unroll(hb, n_chunks)
    lane = lax.broadcasted_iota(jnp.int32, (1, RET_HEADS), 1)
    lg_f_all = jax.nn.log_sigmoid(df_ref[...].astype(F32))
    lg_b_all = jax.nn.log_sigmoid(db_ref[...].astype(F32))
    ri = lax.broadcasted_iota(jnp.int32, (C, C), 0).astype(F32)
    ci = lax.broadcasted_iota(jnp.int32, (C, C), 1).astype(F32)
    dist = ri - ci
    rk = lax.broadcasted_iota(jnp.int32, (C, RET_DK), 0).astype(F32)

    def head_consts(j):
        head = pl.program_id(1) * hb + j
        lgf = jnp.sum(jnp.where(lane == head, lg_f_all, 0.0), axis=-1, keepdims=True)
        lgb = jnp.sum(jnp.where(lane == head, lg_b_all, 0.0), axis=-1, keepdims=True)
        decay = jnp.where(dist >= 0, jnp.exp(lgf * jnp.maximum(dist, 0.0)),
                          jnp.exp(lgb * jnp.maximum(-dist, 0.0)))
        return dict(decay=decay,
                    wq_f=jnp.exp(lgf * (rk + 1.0)), wq_b=jnp.exp(lgb * (C - rk)),
                    wk_f=jnp.exp(lgf * (C - 1.0 - rk)), wk_b=jnp.exp(lgb * rk),
                    gc_f=jnp.exp(lgf * C), gc_b=jnp.exp(lgb * C))

    consts = [head_consts(j) for j in range(hb)]

    def rows(c):
        return pl.ds(pl.multiple_of(c * C, C), C)

    def cols(j, width):
        return slice(j * width, (j + 1) * width)

    def run(streams):
        live = list(streams)
        while live:
            live = [g for g in live if next(g, StopIteration) is not StopIteration]

    def kv_chunk(j, c, wk):
        kw = (k_ref[0, rows(c), cols(j, RET_DK)].astype(F32) * wk).astype(BF)
        return lax.dot_general(kw, v_ref[0, rows(c), cols(j, RET_DV)], (((0,), (0,)), ((), ())),
                               preferred_element_type=F32)

    def scan_body(g, carry):
        state = list(carry)
        kvs = {}

        def products(i):
            n = g * unroll + i
            for j in range(hb):
                kvs[i, j] = (kv_chunk(j, n, consts[j]["wk_f"]), kv_chunk(j, n_chunks - 1 - n, consts[j]["wk_b"]))

        def updates(i):
            n = g * unroll + i
            for j in range(hb):
                sf, sb = state[2 * j], state[2 * j + 1]
                sf_scr[j, n] = sf.astype(BF)
                sb_scr[j, n_chunks - 1 - n] = sb.astype(BF)
                state[2 * j] = consts[j]["gc_f"] * sf + kvs[i, j][0]
                state[2 * j + 1] = consts[j]["gc_b"] * sb + kvs[i, j][1]

        products(0)
        for i in range(1, unroll):
            products(i)
            updates(i - 1)
        updates(unroll - 1)
        return tuple(state)

    init = []
    for j in range(hb):
        if has_s0:
            init += [s0f_ref[j].astype(F32), s0b_ref[j].astype(F32)]
        else:
            init += [jnp.zeros((RET_DK, RET_DV), F32)] * 2
    final = lax.fori_loop(0, n_chunks // unroll, scan_body, tuple(init))
    if emit_state:
        for j in range(hb):
            sfo_ref[j] = final[2 * j]
            sbo_ref[j] = final[2 * j + 1]

    def mix_chunk(j, c, buf, idx):
        k = consts[j]
        q = q_ref[0, rows(c), cols(j, RET_DK)]
        qf32 = q.astype(F32)
        v = v_ref[0, rows(c), cols(j, RET_DV)]
        sc = lax.dot_general(q, k_ref[0, rows(c), cols(j, RET_DK)], (((1,), (1,)), ((), ())),
                             preferred_element_type=F32)
        yield
        o = jnp.dot((sc * k["decay"]).astype(BF), v, preferred_element_type=F32)
        o = o + jnp.dot((qf32 * k["wq_f"]).astype(BF), sf_scr[j, c], preferred_element_type=F32)
        o = o + jnp.dot((qf32 * k["wq_b"]).astype(BF), sb_scr[j, c], preferred_element_type=F32)
        o_scr[buf, idx] = o

    def norm_chunk(j, c, buf, idx):
        o = o_scr[buf, idx]
        mu = jnp.mean(o, axis=-1, keepdims=True)
        yield
        oc = o - mu
        var = jnp.mean(oc * oc, axis=-1, keepdims=True)
        yield
        y = oc * lax.rsqrt(var + GN_EPS) * gn_ref[:, cols(j, RET_DV)]
        gate = g_ref[0, rows(c), cols(j, RET_DV)].astype(F32)
        y_ref[0, rows(c), cols(j, RET_DV)] = (gate * y).astype(BF)

    def group(fn, g, buf):
        return [fn(j, g * unroll + i, buf, i * hb + j) for i in range(unroll) for j in range(hb)]

    n_groups = n_chunks // unroll
    run(group(mix_chunk, 0, 0))
    if n_groups > 1:
        assert n_groups % 2 == 0
        run(group(mix_chunk, 1, 1) + group(norm_chunk, 0, 0))

        def out_body(u, _):
            run(group(mix_chunk, 2 * u, 0) + group(norm_chunk, 2 * u - 1, 1))
            run(group(mix_chunk, 2 * u + 1, 1) + group(norm_chunk, 2 * u, 0))
            return 0

        lax.fori_loop(1, n_groups // 2, out_body, 0)
    run(group(norm_chunk, n_groups - 1, (n_groups - 1) % 2))


def _retention(proj, decay_f, decay_b, gn_g, s0_f, s0_b, *, emit_state, hb):
    bx, lx, _ = proj.shape
    n_chunks = lx // RET_CHUNK
    has_s0 = s0_f is not None
    groups = RET_HEADS // hb
    k_off = (RET_HEADS * RET_DK) // (hb * RET_DK)
    v_off = (2 * RET_HEADS * RET_DK) // (hb * RET_DV)
    g_off = v_off + groups
    state_spec = pl.BlockSpec((None, None, hb, RET_DK, RET_DV), lambda b, h: (b, 0, h, 0, 0))
    in_specs = [pl.BlockSpec((1, lx, hb * RET_DK), lambda b, h: (b, 0, h)),
                pl.BlockSpec((1, lx, hb * RET_DK), lambda b, h: (b, 0, k_off + h)),
                pl.BlockSpec((1, lx, hb * RET_DV), lambda b, h: (b, 0, v_off + h)),
                pl.BlockSpec((1, lx, hb * RET_DV), lambda b, h: (b, 0, g_off + h)),
                pl.BlockSpec((1, RET_HEADS), lambda b, h: (0, 0)),
                pl.BlockSpec((1, RET_HEADS), lambda b, h: (0, 0)),
                pl.BlockSpec((1, hb * RET_DV), lambda b, h: (0, h))]
    args = [proj, proj, proj, proj, decay_f, decay_b, gn_g]
    if has_s0:
        in_specs += [state_spec, state_spec]
        args += [s0_f, s0_b]
    y_shape = jax.ShapeDtypeStruct((bx, lx, RET_HEADS * RET_DV), BF)
    y_spec = pl.BlockSpec((1, lx, hb * RET_DV), lambda b, h: (b, 0, h))
    if emit_state:
        st_shape = jax.ShapeDtypeStruct((bx, 1, RET_HEADS, RET_DK, RET_DV), F32)
        out_shape, out_specs = (y_shape, st_shape, st_shape), (y_spec, state_spec, state_spec)
    else:
        out_shape, out_specs = y_shape, y_spec
    snap = pltpu.VMEM((hb, n_chunks, RET_DK, RET_DV), BF)
    return pl.pallas_call(
        functools.partial(_ret_kernel, n_chunks=n_chunks, hb=hb, has_s0=has_s0, emit_state=emit_state),
        out_shape=out_shape,
        grid=(bx, groups),
        in_specs=in_specs,
        out_specs=out_specs,
        scratch_shapes=[snap, snap,
                        pltpu.VMEM((2, _ret_unroll(hb, n_chunks) * hb, RET_CHUNK, RET_DV), F32)],
        compiler_params=_params(2),
        name="retention_latent" if has_s0 else "retention_ctx",
    )(*args)


def _attn_kernel(*refs, lx, lk_ctx, bq, hb, carry):
    it = iter(refs)
    q_ref, k_ref, v_ref = next(it), next(it), next(it)
    kctx_ref, vctx_ref = (next(it), next(it)) if lk_ctx else (None, None)
    qn_ref, kn_ref, kctxn_ref = (next(it), next(it), next(it)) if carry else (None, None, None)
    lq1_ref, lk1_ref, lq2_ref, lk2_ref, sg_ref = (next(it) for _ in range(5))
    o_ref = next(it)
    vt, s_scr, m_scr, acc_scr = next(it), next(it), next(it), next(it)
    kctx_scr = next(it) if lk_ctx else None
    lk = lx + lk_ctx
    n_kc = lk // ATT_KC
    n_q = lx // bq

    def cols(j):
        return slice(j * LANES, (j + 1) * LANES)

    def chunk(c):
        return slice(c * ATT_KC, (c + 1) * ATT_KC)

    def keys(k_src, kctx_slot, j, c):
        lo = c * ATT_KC
        if lo < lx:
            return k_src[0, lo:lo + ATT_KC, cols(j)]
        return kctx_scr[kctx_slot, lo - lx:lo - lx + ATT_KC, :]

    if carry:
        kctx_scr[hb] = kctxn_ref[0, cols(0), :].T.astype(BF)
    for j in range(hb):
        if lk_ctx:
            kctx_scr[j] = kctx_ref[0, cols(j), :].T.astype(BF)
        for c in range(n_kc):
            lo = c * ATT_KC
            if lo < lx:
                vc = v_ref[0, lo:lo + ATT_KC, cols(j)].astype(F32)
            else:
                vc = vctx_ref[0, lo - lx:lo - lx + ATT_KC, cols(j)]
            vt[j, 0:LANES, chunk(c)] = vc.T.astype(BF)
        vt[j, LANES:ATT_VROWS, :] = jnp.ones((ATT_VROWS - LANES, lk), BF)

    def lam_term(a_ref, b_ref):
        return jnp.exp(jnp.sum(a_ref[...].astype(F32) * b_ref[...].astype(F32), axis=-1, keepdims=True))

    lam = lam_term(lq1_ref, lk1_ref) - lam_term(lq2_ref, lk2_ref) + LAM_INIT
    lane = lax.broadcasted_iota(jnp.int32, (bq, LANES), 1)

    def rows(t):
        return pl.ds(pl.multiple_of(t * bq, bq), bq)

    def scores(j, t, slot, following=False):
        q_src, k_src, kctx_slot = (qn_ref, kn_ref, hb) if following else (q_ref, k_ref, j)
        q = q_src[0, rows(t), cols(j)]
        zero = jnp.zeros_like(q)
        q_maps = (jnp.where(lane < DIFF_DH, q, zero), jnp.where(lane >= DIFF_DH, q, zero))
        m = [jnp.full((8, bq), -jnp.inf, F32) for _ in range(2)]
        for c in range(n_kc):
            kk = keys(k_src, kctx_slot, j, c)
            for i in range(2):
                s = lax.dot_general(kk, q_maps[i], (((1,), (1,)), ((), ())),
                                    preferred_element_type=F32)
                s_scr[j, slot, i, chunk(c), :] = s
                m[i] = jnp.maximum(m[i], jnp.max(s.reshape(ATT_KC // 8, 8, bq), axis=0))
            yield
        for i in range(2):
            m_scr[j, slot, i] = jnp.broadcast_to(jnp.max(m[i], axis=0, keepdims=True), (8, bq))

    def values(j, slot):
        m = [m_scr[j, slot, i][0:1, :] for i in range(2)]
        acc = [jnp.zeros((ATT_VROWS, bq), F32) for _ in range(2)]
        for c in range(n_kc):
            for i in range(2):
                e = jnp.exp2(s_scr[j, slot, i, chunk(c), :] - m[i]).astype(BF)
                acc[i] = acc[i] + jnp.dot(vt[j, :, chunk(c)], e, preferred_element_type=F32)
            yield
        for i in range(2):
            acc_scr[j, slot, i] = acc[i]

    def finish(j, t, slot):
        outs = [acc_scr[j, slot, i, 0:LANES, :] / acc_scr[j, slot, i, LANES:LANES + 1, :] for i in range(2)]
        ot = outs[0] - lam * outs[1]
        ms = jnp.mean(ot * ot, axis=0, keepdims=True)
        o = (ot * lax.rsqrt(ms + RMS_EPS)).T * sg_ref[...]
        o_ref[0, rows(t), cols(j)] = (o * (1.0 - LAM_INIT)).astype(BF)
        yield

    def chain(*streams):
        for g in streams:
            yield from g

    def run(*streams):
        live = list(streams)
        while live:
            live = [g for g in live if next(g, StopIteration) is not StopIteration]

    if n_q == 1:
        run(*[chain(scores(j, 0, 0), values(j, 0), finish(j, 0, 0)) for j in range(hb)])
        return
    assert hb == 1 and n_q % 2 == 0 and n_q >= 4 and carry
    @pl.when((pl.program_id(0) == 0) & (pl.program_id(1) == 0))
    def _():
        run(scores(0, 0, 0))

    run(scores(0, 1, 1), values(0, 0))
    run(scores(0, 2, 0), chain(finish(0, 0, 0), values(0, 1)))

    def body(u, _):
        t = 2 * u
        run(scores(0, t + 1, 1), chain(finish(0, t - 1, 1), values(0, 0)))
        run(scores(0, t + 2, 0), chain(finish(0, t, 0), values(0, 1)))
        return 0

    lax.fori_loop(1, (n_q - 2) // 2, body, 0)
    run(scores(0, n_q - 1, 1), chain(finish(0, n_q - 3, 1), values(0, 0)))
    run(scores(0, 0, 0, following=True), chain(finish(0, n_q - 2, 0), values(0, 1)))
    run(finish(0, n_q - 1, 1))


def _attention(proj, kt_ctx, v_ctx, lam_params, subln_g, *, bq, hb):
    bx, lx, _ = proj.shape
    lk_ctx = 0 if kt_ctx is None else kt_ctx.shape[2]
    lk = lx + lk_ctx
    head_w = 2 * DIFF_DH
    width = hb * head_w
    q_off = 3 * SEG // width
    k_off = 4 * SEG // width
    v_off = 5 * SEG // width

    def heads(off):
        return pl.BlockSpec((1, lx, width), lambda b, h: (b, 0, off + h))

    in_specs = [heads(q_off), heads(k_off), heads(v_off)]
    args = [proj, proj, proj]
    scratch = [pltpu.VMEM((hb, ATT_VROWS, lk), BF),
               pltpu.VMEM((hb, 2, 2, lk, bq), F32), pltpu.VMEM((hb, 2, 2, 8, bq), F32),
               pltpu.VMEM((hb, 2, 2, ATT_VROWS, bq), F32)]
    carry = lx // bq > 1
    if lk_ctx:
        assert lk_ctx % ATT_KC == 0
        in_specs += [pl.BlockSpec((1, width, lk_ctx), lambda b, h: (b, h, 0)),
                     pl.BlockSpec((1, lk_ctx, width), lambda b, h: (b, 0, h))]
        args += [kt_ctx, v_ctx]
        scratch.append(pltpu.VMEM((hb + carry, lk_ctx, head_w), BF))
    if carry:
        assert hb == 1 and lk_ctx
        last = bx * DIFF_HEADS - 1

        def following(b, h):
            g = jnp.minimum(b * DIFF_HEADS + h + 1, last)
            return g // DIFF_HEADS, g % DIFF_HEADS

        in_specs += [pl.BlockSpec((1, lx, width), lambda b, h: (following(b, h)[0], 0, q_off + following(b, h)[1])),
                     pl.BlockSpec((1, lx, width), lambda b, h: (following(b, h)[0], 0, k_off + following(b, h)[1])),
                     pl.BlockSpec((1, width, lk_ctx), lambda b, h: (following(b, h)[0], following(b, h)[1], 0))]
        args += [proj, proj, kt_ctx]
    small = pl.BlockSpec((1, DIFF_DH), lambda b, h: (0, 0))
    in_specs += [small] * 4 + [pl.BlockSpec((1, head_w), lambda b, h: (0, 0))]
    args += list(lam_params) + [subln_g]
    return pl.pallas_call(
        functools.partial(_attn_kernel, lx=lx, lk_ctx=lk_ctx, bq=bq, hb=hb, carry=carry),
        out_shape=jax.ShapeDtypeStruct((bx, lx, DIFF_HEADS * head_w), BF),
        grid=(bx, DIFF_HEADS // hb),
        in_specs=in_specs,
        out_specs=heads(0),
        scratch_shapes=scratch,
        compiler_params=_params(2),
        name="diff_attn_latent" if lk_ctx else "diff_attn_ctx",
    )(*args)


def _post_kernel(x_ref, yr_ref, od_ref, gr_ref, gd_ref, gate1_ref, sh2_ref, sc2_ref, gate2_ref,
                 n2_ref, fn_ref, wr_ref, wd_ref, wo_ref, wg_ref, wu_ref, wdn_ref, o_ref, *, tm):
    def rows_stream(rs):
        br = jnp.dot(yr_ref[0, rs, :], wr_ref[...], preferred_element_type=F32)
        bd = jnp.dot(od_ref[0, rs, :], wd_ref[...], preferred_element_type=F32)
        yield
        merged = (gr_ref[0, rs, :].astype(F32) * br + gd_ref[0, rs, :].astype(F32) * bd).astype(BF)
        mix = jnp.dot(merged, wo_ref[...], preferred_element_type=F32)
        yield
        x1 = x_ref[0, rs, :] + gate1_ref[0] * mix
        ms = jnp.mean(x1 * x1, axis=-1, keepdims=True)
        h2 = x1 * lax.rsqrt(ms + RMS_EPS) * n2_ref[...]
        h2 = (h2 * (1.0 + sc2_ref[0]) + sh2_ref[0]).astype(BF)
        ffn = None
        for lo, hi in FFN_SPLITS:
            g = jnp.dot(h2, wg_ref[:, lo:hi], preferred_element_type=F32)
            u = jnp.dot(h2, wu_ref[:, lo:hi], preferred_element_type=F32)
            yield
            act = (g * jax.nn.sigmoid(g) * u).astype(BF)
            part = jnp.dot(act, wdn_ref[lo:hi, :], preferred_element_type=F32)
            ffn = part if ffn is None else ffn + part
            yield
        x2 = x1 + gate2_ref[0] * ffn
        ms2 = jnp.mean(x2 * x2, axis=-1, keepdims=True)
        o_ref[0, rs, :] = x2 * lax.rsqrt(ms2 + RMS_EPS) * fn_ref[...]

    def delayed(stream, stages):
        for _ in range(stages):
            yield
        yield from stream

    n_groups = tm // POST_ROWS
    live = [delayed(rows_stream(slice(r * POST_ROWS, (r + 1) * POST_ROWS)), r) for r in range(n_groups)]
    while live:
        live = [g for g in live if next(g, StopIteration) is not StopIteration]


def _post(x, y_r, o_d, proj, mod3, norm2_g, final_g, weights, *, latent, tm):
    bx, lx, _ = x.shape
    row = (lambda b: b) if latent else (lambda b: CTX_ROW)

    def tok(width, col):
        return pl.BlockSpec((1, tm, width), lambda b, m: (b, m, col))

    def mod_spec(k):
        return pl.BlockSpec((1, 1, D_MODEL), lambda b, m: (row(b), 0, k))

    def resident(shape):
        return pl.BlockSpec(shape, lambda b, m: (0, 0), pipeline_mode=pl.Buffered(1))

    vec = pl.BlockSpec((1, D_MODEL), lambda b, m: (0, 0))
    in_specs = [tok(D_MODEL, 0), tok(D_MODEL, 0), tok(D_MODEL, 0), tok(SEG, 6), tok(SEG, 7),
                mod_spec(2), mod_spec(3), mod_spec(4), mod_spec(5), vec, vec]
    in_specs += [resident(w.shape) for w in weights]
    return pl.pallas_call(
        functools.partial(_post_kernel, tm=tm),
        out_shape=jax.ShapeDtypeStruct((bx, lx, D_MODEL), F32),
        grid=(bx, lx // tm),
        in_specs=in_specs,
        out_specs=tok(D_MODEL, 0),
        compiler_params=_params(2),
        name="post_latent" if latent else "post_ctx",
    )(x, y_r, o_d, proj, proj, mod3, mod3, mod3, mod3, norm2_g, final_g, *weights)


def kernel(x_prompt, x_sample, state_ret_fwd, state_ret_bwd, cache_diff_k, cache_diff_v, c, c_ctx,
           norm1_g, norm2_g, w_ada, b_ada, w_in, b_gate, ret_decay_fwd, ret_decay_bwd, ret_gn_g,
           w_ret_out, diff_lambda_q1, diff_lambda_k1, diff_lambda_q2, diff_lambda_k2, diff_subln_g,
           w_diff_out, w_o, w_ffn_gate, w_ffn_up, w_ffn_down, final_norm_g):
    batch, seq, _ = x_prompt.shape
    dec_batch, dec_seq, _ = x_sample.shape
    past_len = cache_diff_k.shape[2]
    layer = 0

    cvec = jnp.zeros((MOD_ROWS, D_MODEL), F32).at[:dec_batch].set(c).at[CTX_ROW].set(c_ctx)
    mod = _modulation(cvec, w_ada[layer], b_ada[layer][None, :])
    mod3 = mod.reshape(MOD_ROWS, 1, 6 * D_MODEL)

    w_in_bf = w_in[layer].astype(BF)
    post_w = tuple(w[layer].astype(BF) for w in (w_ret_out, w_diff_out, w_o, w_ffn_gate, w_ffn_up, w_ffn_down))
    n1 = norm1_g[layer][None, :]
    n2 = norm2_g[layer][None, :]
    fn = final_norm_g[None, :]
    bg = b_gate[layer][None, :]
    dec_f = ret_decay_fwd[layer][None, :]
    dec_b = ret_decay_bwd[layer][None, :]
    gn = ret_gn_g[layer][None, :]
    lam_params = tuple(p[layer][None, :] for p in (diff_lambda_q1, diff_lambda_k1, diff_lambda_q2, diff_lambda_k2))
    subln = diff_subln_g[layer][None, :]

    xp_flat = x_prompt.reshape(1, batch * seq, D_MODEL)
    assert seq == PROJ_ROWS
    proj_c, kt_new, v_new = _inproj(xp_flat, mod3, n1, w_in_bf, bg, None, latent=False, tm=512)
    proj_c_b = proj_c.reshape(batch, seq, W_IN_COLS)
    yr_c, s_f, s_b = _retention(proj_c_b, dec_f, dec_b, gn, None, None, emit_state=True, hb=RET_HEADS)
    od_c = _attention(proj_c_b, None, None, lam_params, subln, bq=seq, hb=DIFF_HEADS)
    y_prompt = _post(xp_flat, yr_c.reshape(1, batch * seq, -1), od_c.reshape(1, batch * seq, -1),
                     proj_c, mod3, n2, fn, post_w, latent=False, tm=512).reshape(batch, seq, D_MODEL)

    rope = _rope_tables(dec_seq)
    proj_l = _inproj(x_sample, mod3, n1, w_in_bf, bg, rope, latent=True, tm=512)
    yr_l = _retention(proj_l, dec_f, dec_b, gn, state_ret_fwd, state_ret_bwd, emit_state=False, hb=2)
    kt_ctx = jnp.transpose(cache_diff_k[:, layer], (0, 2, 3, 4, 1)).reshape(dec_batch, -1, past_len)
    v_ctx = cache_diff_v[:, layer].reshape(dec_batch, past_len, DIFF_HEADS * 2 * DIFF_DH)
    od_l = _attention(proj_l, kt_ctx, v_ctx, lam_params, subln, bq=256, hb=1)
    y_sample = _post(x_sample, yr_l, od_l, proj_l, mod3, n2, fn, post_w, latent=True, tm=512)

    new_diff_k = jnp.transpose(kt_new.reshape(batch, DIFF_HEADS, 2, DIFF_DH, seq), (0, 4, 1, 2, 3))[:, None]
    new_diff_v = v_new.reshape(batch, 1, seq, DIFF_HEADS, 2 * DIFF_DH)
    return (y_prompt, y_sample, s_f, s_b, new_diff_k, new_diff_v)
```

```python
import functools
import math

import jax
import jax.numpy as jnp
import numpy as np
from jax import lax
from jax.experimental import pallas as pl
from jax.experimental.pallas import tpu as pltpu

D_MODEL = 1024
GRID_W = 64
ROPE_BASE = 10000.0
RET_HEADS = 4
RET_DK = 128
RET_DV = 256
RET_CHUNK = 128
DIFF_HEADS = 8
DIFF_DH = 64
FFN_HIDDEN = 2816
W_IN_COLS = 8192
RMS_EPS = 1e-6
GN_EPS = 1e-5
LAM_INIT = 0.8 - 0.6 * math.exp(-0.3 * 0)

LANES = 128
SUBLANES = 8
BF16_SUBLANE_TILE = 16
MXU_DIM = 256
VMEM_LIMIT = 56 * 1024 * 1024

SEG = 1024
MOD_ROWS = SUBLANES
CTX_ROW = 4
K_R_SCALE = RET_DK ** -0.5
Q_D_SCALE = DIFF_DH ** -0.5 * math.log2(math.e)
FFN_SPLITS = ((0, 6 * MXU_DIM), (6 * MXU_DIM, FFN_HIDDEN))

MOD_TN = 2048
TOKEN_TILE = 512
PROJ_ROWS = 256
POST_ROWS = 256
ATT_BQ = 256
ATT_KC = MXU_DIM
ATT_HEAD_STREAMS = 4
ATT_VROWS = LANES + BF16_SUBLANE_TILE
RET_CHAINS = 16
RET_HB_LATENT = 2

BF = jnp.bfloat16
F32 = jnp.float32


def _params(n_axes):
    return pltpu.CompilerParams(dimension_semantics=("arbitrary",) * n_axes,
                                vmem_limit_bytes=VMEM_LIMIT)


def _mod_kernel(c_ref, w_ref, b_ref, o_ref):
    c = c_ref[...]
    s = c * jax.nn.sigmoid(c)
    o_ref[...] = jnp.dot(s, w_ref[...], preferred_element_type=F32) + b_ref[...]


def _modulation(cvec, w_ada, b_ada):
    n_out = w_ada.shape[1]
    tn = MOD_TN
    return pl.pallas_call(
        _mod_kernel,
        out_shape=jax.ShapeDtypeStruct((MOD_ROWS, n_out), F32),
        grid=(n_out // tn,),
        in_specs=[pl.BlockSpec((MOD_ROWS, D_MODEL), lambda n: (0, 0)),
                  pl.BlockSpec((D_MODEL, tn), lambda n: (0, n)),
                  pl.BlockSpec((1, tn), lambda n: (0, n))],
        out_specs=pl.BlockSpec((MOD_ROWS, tn), lambda n: (0, n)),
        compiler_params=_params(1),
        name="modulation",
    )(cvec, w_ada, b_ada)


def _rope_tables(seq_len):
    rows = seq_len // GRID_W
    pos_r = np.repeat(np.arange(rows), GRID_W).astype(np.float64)
    pos_c = np.tile(np.arange(GRID_W), rows).astype(np.float64)

    def angles(d):
        n = d // 4
        inv = ROPE_BASE ** (-np.arange(n, dtype=np.float64) / n)
        return np.concatenate([pos_r[:, None] * inv, pos_c[:, None] * inv], axis=-1)

    ang_r = angles(RET_DK)
    cos_r = np.concatenate([np.cos(ang_r)] * 2, axis=-1)
    sin_r = np.concatenate([-np.sin(ang_r), np.sin(ang_r)], axis=-1)
    ang_d = angles(DIFF_DH)
    zero = np.zeros_like(ang_d)
    cos_d = np.concatenate([np.cos(ang_d)] * 4, axis=-1)
    sin_lo = np.concatenate([zero, np.sin(ang_d)] * 2, axis=-1)
    sin_hi = np.concatenate([-np.sin(ang_d), zero] * 2, axis=-1)
    return tuple(jnp.asarray(t, dtype=F32) for t in (cos_r, sin_r, cos_d, sin_lo, sin_hi))


def _inproj_kernel(*refs, latent, tm):
    if latent:
        (x_ref, sh_ref, sc_ref, g1_ref, w_ref, bg_ref,
         cr_ref, sr_ref, cd_ref, slo_ref, shi_ref, o_ref) = refs
    else:
        x_ref, sh_ref, sc_ref, g1_ref, w_ref, bg_ref, o_ref, ko_ref, vo_ref = refs
    heads = SEG // LANES

    def seg(n):
        return slice(n * SEG, (n + 1) * SEG)

    def normed(rs):
        x = x_ref[0, rs, :]
        ms = jnp.mean(x * x, axis=-1, keepdims=True)
        y = x * lax.rsqrt(ms + RMS_EPS) * g1_ref[...]
        return (y * (1.0 + sc_ref[0]) + sh_ref[0]).astype(BF)

    def rope_r(a, rs):
        return a * cr_ref[rs, :] + pltpu.roll(a, 64, 1) * sr_ref[rs, :]

    def rope_d(a, rs):
        return (a * cd_ref[rs, :] + pltpu.roll(a, 32, 1) * slo_ref[rs, :]
                + pltpu.roll(a, 96, 1) * shi_ref[rs, :])

    def per_head(rs, n, acc, fn):
        for j in range(heads):
            lo = n * SEG + j * LANES
            o_ref[0, rs, lo:lo + LANES] = fn(acc[:, j * LANES:(j + 1) * LANES], j).astype(BF)

    for r in range(tm // PROJ_ROWS):
        rs = slice(r * PROJ_ROWS, (r + 1) * PROJ_ROWS)
        h = normed(rs)

        def project(n):
            return jnp.dot(h, w_ref[:, seg(n)], preferred_element_type=F32)

        def q_or_k_r(a, j):
            a = rope_r(a, rs) if latent else a
            return a * K_R_SCALE if j >= heads // 2 else a

        per_head(rs, 0, project(0), q_or_k_r)
        o_ref[0, rs, seg(1)] = project(1).astype(BF)
        acc = project(2)
        o_ref[0, rs, seg(2)] = (acc * jax.nn.sigmoid(acc)).astype(BF)
        acc = project(3)
        if latent:
            per_head(rs, 3, acc, lambda a, j: rope_d(a, rs) * Q_D_SCALE)
        else:
            o_ref[0, rs, seg(3)] = (acc * Q_D_SCALE).astype(BF)
        acc = project(4)
        if latent:
            per_head(rs, 4, acc, lambda a, j: rope_d(a, rs))
        else:
            o_ref[0, rs, seg(4)] = acc.astype(BF)
            ko_ref[r] = acc.T
        acc = project(5)
        o_ref[0, rs, seg(5)] = acc.astype(BF)
        if not latent:
            vo_ref[0, rs, :] = acc
        for n in (6, 7):
            gate_cols = slice((n - 6) * SEG, (n - 5) * SEG)
            o_ref[0, rs, seg(n)] = jax.nn.sigmoid(project(n) + bg_ref[:, gate_cols]).astype(BF)


def _inproj(x, mod3, norm_g, w_in_bf, b_gate, rope, *, latent, tm):
    bx, lx, _ = x.shape
    row = (lambda b: b) if latent else (lambda b: CTX_ROW)

    def mod_spec(k):
        return pl.BlockSpec((1, 1, D_MODEL), lambda b, m: (row(b), 0, k))

    def resident(shape):
        return pl.BlockSpec(shape, lambda b, m: (0, 0), pipeline_mode=pl.Buffered(1))

    in_specs = [pl.BlockSpec((1, tm, D_MODEL), lambda b, m: (b, m, 0)),
                mod_spec(0), mod_spec(1),
                resident(norm_g.shape), resident(w_in_bf.shape), resident(b_gate.shape)]
    args = [x, mod3, mod3, norm_g, w_in_bf, b_gate]
    proj_shape = jax.ShapeDtypeStruct((bx, lx, W_IN_COLS), BF)
    proj_spec = pl.BlockSpec((1, tm, W_IN_COLS), lambda b, m: (b, m, 0))
    if latent:
        in_specs += [pl.BlockSpec((tm, LANES), lambda b, m: (m, 0))] * 5
        args += list(rope)
        out_shape, out_specs = proj_shape, proj_spec
    else:
        v_shape = jax.ShapeDtypeStruct((bx, lx, SEG), F32)
        v_spec = pl.BlockSpec((1, tm, SEG), lambda b, m: (b, m, 0))
        kt_shape = jax.ShapeDtypeStruct((bx * lx // PROJ_ROWS, SEG, PROJ_ROWS), F32)
        kt_spec = pl.BlockSpec((tm // PROJ_ROWS, SEG, PROJ_ROWS), lambda b, m: (b * (lx // tm) + m, 0, 0))
        out_shape, out_specs = (proj_shape, kt_shape, v_shape), (proj_spec, kt_spec, v_spec)
    return pl.pallas_call(
        functools.partial(_inproj_kernel, latent=latent, tm=tm),
        out_shape=out_shape,
        grid=(bx, lx // tm),
        in_specs=in_specs,
        out_specs=out_specs,
        compiler_params=_params(2),
        name="inproj_latent" if latent else "inproj_ctx",
    )(*args)


def _ret_unroll(hb, n_chunks):
    return max(1, min(RET_CHAINS // hb, n_chunks))


def _ret_kernel(*refs, n_chunks, hb, has_s0, emit_state):
    it = iter(refs)
    q_ref, k_ref, v_ref, g_ref, df_ref, db_ref, gn_ref = (next(it) for _ in range(7))
    s0f_ref, s0b_ref = (next(it), next(it)) if has_s0 else (None, None)
    y_ref = next(it)
    sfo_ref, sbo_ref = (next(it), next(it)) if emit_state else (None, None)
    sf_scr, sb_scr, o_scr = next(it), next(it), next(it)
    C = RET_CHUNK
    unroll = _ret_unroll(hb, n_chunks)
    lane = lax.broadcasted_iota(jnp.int32, (1, RET_HEADS), 1)
    lg_f_all = jax.nn.log_sigmoid(df_ref[...].astype(F32))
    lg_b_all = jax.nn.log_sigmoid(db_ref[...].astype(F32))
    ri = lax.broadcasted_iota(jnp.int32, (C, C), 0).astype(F32)
    ci = lax.broadcasted_iota(jnp.int32, (C, C), 1).astype(F32)
    dist = ri - ci
    rk = lax.broadcasted_iota(jnp.int32, (C, RET_DK), 0).astype(F32)

    def head_consts(j):
        head = pl.program_id(1) * hb + j
        lgf = jnp.sum(jnp.where(lane == head, lg_f_all, 0.0), axis=-1, keepdims=True)
        lgb = jnp.sum(jnp.where(lane == head, lg_b_all, 0.0), axis=-1, keepdims=True)
        decay = jnp.where(dist >= 0, jnp.exp(lgf * jnp.maximum(dist, 0.0)),
                          jnp.exp(lgb * jnp.maximum(-dist, 0.0)))
        return dict(decay=decay,
                    wq_f=jnp.exp(lgf * (rk + 1.0)), wq_b=jnp.exp(lgb * (C - rk)),
                    wk_f=jnp.exp(lgf * (C - 1.0 - rk)), wk_b=jnp.exp(lgb * rk),
                    gc_f=jnp.exp(lgf * C), gc_b=jnp.exp(lgb * C))

    consts = [head_consts(j) for j in range(hb)]

    def rows(c):
        return pl.ds(pl.multiple_of(c * C, C), C)

    def cols(j, width):
        return slice(j * width, (j + 1) * width)

    def run(streams):
        live = list(streams)
        while live:
            live = [g for g in live if next(g, StopIteration) is not StopIteration]

    def kv_chunk(j, c, wk):
        kw = (k_ref[0, rows(c), cols(j, RET_DK)].astype(F32) * wk).astype(BF)
        return lax.dot_general(kw, v_ref[0, rows(c), cols(j, RET_DV)], (((0,), (0,)), ((), ())),
                               preferred_element_type=F32)

    def scan_body(g, carry):
        state = list(carry)
        kvs = {}

        def products(i):
            n = g * unroll + i
            for j in range(hb):
                kvs[i, j] = (kv_chunk(j, n, consts[j]["wk_f"]), kv_chunk(j, n_chunks - 1 - n, consts[j]["wk_b"]))

        def updates(i):
            n = g * unroll + i
            for j in range(hb):
                sf, sb = state[2 * j], state[2 * j + 1]
                sf_scr[j, n] = sf.astype(BF)
                sb_scr[j, n_chunks - 1 - n] = sb.astype(BF)
                state[2 * j] = consts[j]["gc_f"] * sf + kvs[i, j][0]
                state[2 * j + 1] = consts[j]["gc_b"] * sb + kvs[i, j][1]

        products(0)
        for i in range(1, unroll):
            products(i)
            updates(i - 1)
        updates(unroll - 1)
        return tuple(state)

    init = []
    for j in range(hb):
        if has_s0:
            init += [s0f_ref[j].astype(F32), s0b_ref[j].astype(F32)]
        else:
            init += [jnp.zeros((RET_DK, RET_DV), F32)] * 2
    final = lax.fori_loop(0, n_chunks // unroll, scan_body, tuple(init))
    if emit_state:
        for j in range(hb):
            sfo_ref[j] = final[2 * j]
            sbo_ref[j] = final[2 * j + 1]

    def mix_chunk(j, c, buf, idx):
        k = consts[j]
        q = q_ref[0, rows(c), cols(j, RET_DK)]
        qf32 = q.astype(F32)
        v = v_ref[0, rows(c), cols(j, RET_DV)]
        sc = lax.dot_general(q, k_ref[0, rows(c), cols(j, RET_DK)], (((1,), (1,)), ((), ())),
                             preferred_element_type=F32)
        yield
        o = jnp.dot((sc * k["decay"]).astype(BF), v, preferred_element_type=F32)
        o = o + jnp.dot((qf32 * k["wq_f"]).astype(BF), sf_scr[j, c], preferred_element_type=F32)
        o = o + jnp.dot((qf32 * k["wq_b"]).astype(BF), sb_scr[j, c], preferred_element_type=F32)
        o_scr[buf, idx] = o

    def norm_chunk(j, c, buf, idx):
        o = o_scr[buf, idx]
        mu = jnp.mean(o, axis=-1, keepdims=True)
        yield
        oc = o - mu
        var = jnp.mean(oc * oc, axis=-1, keepdims=True)
        yield
        y = oc * lax.rsqrt(var + GN_EPS) * gn_ref[:, cols(j, RET_DV)]
        gate = g_ref[0, rows(c), cols(j, RET_DV)].astype(F32)
        y_ref[0, rows(c), cols(j, RET_DV)] = (gate * y).astype(BF)

    def group(fn, g, buf):
        return [fn(j, g * unroll + i, buf, i * hb + j) for i in range(unroll) for j in range(hb)]

    n_groups = n_chunks // unroll
    run(group(mix_chunk, 0, 0))
    if n_groups > 1:
        assert n_groups % 2 == 0
        run(group(mix_chunk, 1, 1) + group(norm_chunk, 0, 0))

        def out_body(u, _):
            run(group(mix_chunk, 2 * u, 0) + group(norm_chunk, 2 * u - 1, 1))
            run(group(mix_chunk, 2 * u + 1, 1) + group(norm_chunk, 2 * u, 0))
            return 0

        lax.fori_loop(1, n_groups // 2, out_body, 0)
    run(group(norm_chunk, n_groups - 1, (n_groups - 1) % 2))


def _retention(proj, decay_f, decay_b, gn_g, s0_f, s0_b, *, emit_state, hb):
    bx, lx, _ = proj.shape
    n_chunks = lx // RET_CHUNK
    has_s0 = s0_f is not None
    groups = RET_HEADS // hb
    k_off = (RET_HEADS * RET_DK) // (hb * RET_DK)
    v_off = (2 * RET_HEADS * RET_DK) // (hb * RET_DV)
    g_off = v_off + groups
    state_spec = pl.BlockSpec((None, None, hb, RET_DK, RET_DV), lambda b, h: (b, 0, h, 0, 0))
    in_specs = [pl.BlockSpec((1, lx, hb * RET_DK), lambda b, h: (b, 0, h)),
                pl.BlockSpec((1, lx, hb * RET_DK), lambda b, h: (b, 0, k_off + h)),
                pl.BlockSpec((1, lx, hb * RET_DV), lambda b, h: (b, 0, v_off + h)),
                pl.BlockSpec((1, lx, hb * RET_DV), lambda b, h: (b, 0, g_off + h)),
                pl.BlockSpec((1, RET_HEADS), lambda b, h: (0, 0)),
                pl.BlockSpec((1, RET_HEADS), lambda b, h: (0, 0)),
                pl.BlockSpec((1, hb * RET_DV), lambda b, h: (0, h))]
    args = [proj, proj, proj, proj, decay_f, decay_b, gn_g]
    if has_s0:
        in_specs += [state_spec, state_spec]
        args += [s0_f, s0_b]
    y_shape = jax.ShapeDtypeStruct((bx, lx, RET_HEADS * RET_DV), BF)
    y_spec = pl.BlockSpec((1, lx, hb * RET_DV), lambda b, h: (b, 0, h))
    if emit_state:
        st_shape = jax.ShapeDtypeStruct((bx, 1, RET_HEADS, RET_DK, RET_DV), F32)
        out_shape, out_specs = (y_shape, st_shape, st_shape), (y_spec, state_spec, state_spec)
    else:
        out_shape, out_specs = y_shape, y_spec
    snap = pltpu.VMEM((hb, n_chunks, RET_DK, RET_DV), BF)
    return pl.pallas_call(
        functools.partial(_ret_kernel, n_chunks=n_chunks, hb=hb, has_s0=has_s0, emit_state=emit_state),
        out_shape=out_shape,
        grid=(bx, groups),
        in_specs=in_specs,
        out_specs=out_specs,
        scratch_shapes=[snap, snap,
                        pltpu.VMEM((2, _ret_unroll(hb, n_chunks) * hb, RET_CHUNK, RET_DV), F32)],
        compiler_params=_params(2),
        name="retention_latent" if has_s0 else "retention_ctx",
    )(*args)


def _attn_kernel(*refs, lx, lk_ctx, bq, hb, carry):
    it = iter(refs)
    q_ref, k_ref, v_ref = next(it), next(it), next(it)
    kctx_ref, vctx_ref = (next(it), next(it)) if lk_ctx else (None, None)
    qn_ref, kn_ref, kctxn_ref = (next(it), next(it), next(it)) if carry else (None, None, None)
    lq1_ref, lk1_ref, lq2_ref, lk2_ref, sg_ref = (next(it) for _ in range(5))
    o_ref = next(it)
    vt, s_scr, m_scr, acc_scr = next(it), next(it), next(it), next(it)
    kctx_scr = next(it) if lk_ctx else None
    lk = lx + lk_ctx
    n_kc = lk // ATT_KC
    n_q = lx // bq

    def cols(j):
        return slice(j * LANES, (j + 1) * LANES)

    def chunk(c):
        return slice(c * ATT_KC, (c + 1) * ATT_KC)

    def keys(k_src, kctx_slot, j, c):
        lo = c * ATT_KC
        if lo < lx:
            return k_src[0, lo:lo + ATT_KC, cols(j)]
        return kctx_scr[kctx_slot, lo - lx:lo - lx + ATT_KC, :]

    if carry:
        kctx_scr[hb] = kctxn_ref[0, cols(0), :].T.astype(BF)
    for j in range(hb):
        if lk_ctx:
            kctx_scr[j] = kctx_ref[0, cols(j), :].T.astype(BF)
        for c in range(n_kc):
            lo = c * ATT_KC
            if lo < lx:
                vc = v_ref[0, lo:lo + ATT_KC, cols(j)].astype(F32)
            else:
                vc = vctx_ref[0, lo - lx:lo - lx + ATT_KC, cols(j)]
            vt[j, 0:LANES, chunk(c)] = vc.T.astype(BF)
        vt[j, LANES:ATT_VROWS, :] = jnp.ones((ATT_VROWS - LANES, lk), BF)

    def lam_term(a_ref, b_ref):
        return jnp.exp(jnp.sum(a_ref[...].astype(F32) * b_ref[...].astype(F32), axis=-1, keepdims=True))

    lam = lam_term(lq1_ref, lk1_ref) - lam_term(lq2_ref, lk2_ref) + LAM_INIT
    lane = lax.broadcasted_iota(jnp.int32, (bq, LANES), 1)

    def rows(t):
        return pl.ds(pl.multiple_of(t * bq, bq), bq)

    def scores(j, t, slot, following=False):
        q_src, k_src, kctx_slot = (qn_ref, kn_ref, hb) if following else (q_ref, k_ref, j)
        q = q_src[0, rows(t), cols(j)]
        zero = jnp.zeros_like(q)
        q_maps = (jnp.where(lane < DIFF_DH, q, zero), jnp.where(lane >= DIFF_DH, q, zero))
        m = [jnp.full((SUBLANES, bq), -jnp.inf, F32) for _ in range(2)]
        for c in range(n_kc):
            kk = keys(k_src, kctx_slot, j, c)
            for i in range(2):
                s = lax.dot_general(kk, q_maps[i], (((1,), (1,)), ((), ())),
                                    preferred_element_type=F32)
                s_scr[j, slot, i, chunk(c), :] = s
                m[i] = jnp.maximum(m[i], jnp.max(s.reshape(ATT_KC // SUBLANES, SUBLANES, bq), axis=0))
            yield
        for i in range(2):
            m_scr[j, slot, i] = jnp.broadcast_to(jnp.max(m[i], axis=0, keepdims=True), (SUBLANES, bq))

    def values(j, slot):
        m = [m_scr[j, slot, i][0:1, :] for i in range(2)]
        acc = [jnp.zeros((ATT_VROWS, bq), F32) for _ in range(2)]
        for c in range(n_kc):
            for i in range(2):
                e = jnp.exp2(s_scr[j, slot, i, chunk(c), :] - m[i]).astype(BF)
                acc[i] = acc[i] + jnp.dot(vt[j, :, chunk(c)], e, preferred_element_type=F32)
            yield
        for i in range(2):
            acc_scr[j, slot, i] = acc[i]

    def finish(j, t, slot):
        outs = [acc_scr[j, slot, i, 0:LANES, :] / acc_scr[j, slot, i, LANES:LANES + 1, :] for i in range(2)]
        ot = outs[0] - lam * outs[1]
        ms = jnp.mean(ot * ot, axis=0, keepdims=True)
        o = (ot * lax.rsqrt(ms + RMS_EPS)).T * sg_ref[...]
        o_ref[0, rows(t), cols(j)] = (o * (1.0 - LAM_INIT)).astype(BF)
        yield

    def chain(*streams):
        for g in streams:
            yield from g

    def run(*streams):
        live = list(streams)
        while live:
            live = [g for g in live if next(g, StopIteration) is not StopIteration]

    if n_q == 1:
        for j0 in range(0, hb, ATT_HEAD_STREAMS):
            run(*[chain(scores(j, 0, 0), values(j, 0), finish(j, 0, 0))
                  for j in range(j0, min(j0 + ATT_HEAD_STREAMS, hb))])
        return
    assert hb == 1 and n_q % 2 == 0 and n_q >= 4 and carry
    @pl.when((pl.program_id(0) == 0) & (pl.program_id(1) == 0))
    def _():
        run(scores(0, 0, 0))

    run(scores(0, 1, 1), values(0, 0))
    run(scores(0, 2, 0), chain(finish(0, 0, 0), values(0, 1)))

    def body(u, _):
        t = 2 * u
        run(scores(0, t + 1, 1), chain(finish(0, t - 1, 1), values(0, 0)))
        run(scores(0, t + 2, 0), chain(finish(0, t, 0), values(0, 1)))
        return 0

    lax.fori_loop(1, (n_q - 2) // 2, body, 0)
    run(scores(0, n_q - 1, 1), chain(finish(0, n_q - 3, 1), values(0, 0)))
    run(scores(0, 0, 0, following=True), chain(finish(0, n_q - 2, 0), values(0, 1)))
    run(finish(0, n_q - 1, 1))


def _attention(proj, kt_ctx, v_ctx, lam_params, subln_g, *, bq, hb):
    bx, lx, _ = proj.shape
    lk_ctx = 0 if kt_ctx is None else kt_ctx.shape[2]
    lk = lx + lk_ctx
    head_w = 2 * DIFF_DH
    width = hb * head_w
    q_off = 3 * SEG // width
    k_off = 4 * SEG // width
    v_off = 5 * SEG // width

    def heads(off):
        return pl.BlockSpec((1, lx, width), lambda b, h: (b, 0, off + h))

    in_specs = [heads(q_off), heads(k_off), heads(v_off)]
    args = [proj, proj, proj]
    scratch = [pltpu.VMEM((hb, ATT_VROWS, lk), BF),
               pltpu.VMEM((hb, 2, 2, lk, bq), F32), pltpu.VMEM((hb, 2, 2, SUBLANES, bq), F32),
               pltpu.VMEM((hb, 2, 2, ATT_VROWS, bq), F32)]
    carry = lx // bq > 1
    if lk_ctx:
        assert lk_ctx % ATT_KC == 0
        in_specs += [pl.BlockSpec((1, width, lk_ctx), lambda b, h: (b, h, 0)),
                     pl.BlockSpec((1, lk_ctx, width), lambda b, h: (b, 0, h))]
        args += [kt_ctx, v_ctx]
        scratch.append(pltpu.VMEM((hb + carry, lk_ctx, head_w), BF))
    if carry:
        assert hb == 1 and lk_ctx
        last = bx * DIFF_HEADS - 1

        def following(b, h):
            g = jnp.minimum(b * DIFF_HEADS + h + 1, last)
            return g // DIFF_HEADS, g % DIFF_HEADS

        in_specs += [pl.BlockSpec((1, lx, width), lambda b, h: (following(b, h)[0], 0, q_off + following(b, h)[1])),
                     pl.BlockSpec((1, lx, width), lambda b, h: (following(b, h)[0], 0, k_off + following(b, h)[1])),
                     pl.BlockSpec((1, width, lk_ctx), lambda b, h: (following(b, h)[0], following(b, h)[1], 0))]
        args += [proj, proj, kt_ctx]
    small = pl.BlockSpec((1, DIFF_DH), lambda b, h: (0, 0))
    in_specs += [small] * 4 + [pl.BlockSpec((1, head_w), lambda b, h: (0, 0))]
    args += list(lam_params) + [subln_g]
    return pl.pallas_call(
        functools.partial(_attn_kernel, lx=lx, lk_ctx=lk_ctx, bq=bq, hb=hb, carry=carry),
        out_shape=jax.ShapeDtypeStruct((bx, lx, DIFF_HEADS * head_w), BF),
        grid=(bx, DIFF_HEADS // hb),
        in_specs=in_specs,
        out_specs=heads(0),
        scratch_shapes=scratch,
        compiler_params=_params(2),
        name="diff_attn_latent" if lk_ctx else "diff_attn_ctx",
    )(*args)


def _post_kernel(x_ref, yr_ref, od_ref, gr_ref, gd_ref, gate1_ref, sh2_ref, sc2_ref, gate2_ref,
                 n2_ref, fn_ref, wr_ref, wd_ref, wo_ref, wg_ref, wu_ref, wdn_ref, o_ref, *, tm):
    def rows_stream(rs):
        br = jnp.dot(yr_ref[0, rs, :], wr_ref[...], preferred_element_type=F32)
        bd = jnp.dot(od_ref[0, rs, :], wd_ref[...], preferred_element_type=F32)
        yield
        merged = (gr_ref[0, rs, :].astype(F32) * br + gd_ref[0, rs, :].astype(F32) * bd).astype(BF)
        mix = jnp.dot(merged, wo_ref[...], preferred_element_type=F32)
        yield
        x1 = x_ref[0, rs, :] + gate1_ref[0] * mix
        ms = jnp.mean(x1 * x1, axis=-1, keepdims=True)
        h2 = x1 * lax.rsqrt(ms + RMS_EPS) * n2_ref[...]
        h2 = (h2 * (1.0 + sc2_ref[0]) + sh2_ref[0]).astype(BF)
        ffn = None
        for lo, hi in FFN_SPLITS:
            g = jnp.dot(h2, wg_ref[:, lo:hi], preferred_element_type=F32)
            u = jnp.dot(h2, wu_ref[:, lo:hi], preferred_element_type=F32)
            yield
            act = (g * jax.nn.sigmoid(g) * u).astype(BF)
            part = jnp.dot(act, wdn_ref[lo:hi, :], preferred_element_type=F32)
            ffn = part if ffn is None else ffn + part
            yield
        x2 = x1 + gate2_ref[0] * ffn
        ms2 = jnp.mean(x2 * x2, axis=-1, keepdims=True)
        o_ref[0, rs, :] = x2 * lax.rsqrt(ms2 + RMS_EPS) * fn_ref[...]

    def delayed(stream, stages):
        for _ in range(stages):
            yield
        yield from stream

    n_groups = tm // POST_ROWS
    live = [delayed(rows_stream(slice(r * POST_ROWS, (r + 1) * POST_ROWS)), r) for r in range(n_groups)]
    while live:
        live = [g for g in live if next(g, StopIteration) is not StopIteration]


def _post(x, y_r, o_d, proj, mod3, norm2_g, final_g, weights, *, latent, tm):
    bx, lx, _ = x.shape
    row = (lambda b: b) if latent else (lambda b: CTX_ROW)

    def tok(width, col):
        return pl.BlockSpec((1, tm, width), lambda b, m: (b, m, col))

    def mod_spec(k):
        return pl.BlockSpec((1, 1, D_MODEL), lambda b, m: (row(b), 0, k))

    def resident(shape):
        return pl.BlockSpec(shape, lambda b, m: (0, 0), pipeline_mode=pl.Buffered(1))

    vec = pl.BlockSpec((1, D_MODEL), lambda b, m: (0, 0))
    in_specs = [tok(D_MODEL, 0), tok(D_MODEL, 0), tok(D_MODEL, 0), tok(SEG, 6), tok(SEG, 7),
                mod_spec(2), mod_spec(3), mod_spec(4), mod_spec(5), vec, vec]
    in_specs += [resident(w.shape) for w in weights]
    return pl.pallas_call(
        functools.partial(_post_kernel, tm=tm),
        out_shape=jax.ShapeDtypeStruct((bx, lx, D_MODEL), F32),
        grid=(bx, lx // tm),
        in_specs=in_specs,
        out_specs=tok(D_MODEL, 0),
        compiler_params=_params(2),
        name="post_latent" if latent else "post_ctx",
    )(x, y_r, o_d, proj, proj, mod3, mod3, mod3, mod3, norm2_g, final_g, *weights)


def kernel(x_prompt, x_sample, state_ret_fwd, state_ret_bwd, cache_diff_k, cache_diff_v, c, c_ctx,
           norm1_g, norm2_g, w_ada, b_ada, w_in, b_gate, ret_decay_fwd, ret_decay_bwd, ret_gn_g,
           w_ret_out, diff_lambda_q1, diff_lambda_k1, diff_lambda_q2, diff_lambda_k2, diff_subln_g,
           w_diff_out, w_o, w_ffn_gate, w_ffn_up, w_ffn_down, final_norm_g):
    batch, seq, _ = x_prompt.shape
    dec_batch, dec_seq, _ = x_sample.shape
    past_len = cache_diff_k.shape[2]
    layer = 0

    cvec = jnp.zeros((MOD_ROWS, D_MODEL), F32).at[:dec_batch].set(c).at[CTX_ROW].set(c_ctx)
    mod = _modulation(cvec, w_ada[layer], b_ada[layer][None, :])
    mod3 = mod.reshape(MOD_ROWS, 1, 6 * D_MODEL)

    w_in_bf = w_in[layer].astype(BF)
    post_w = tuple(w[layer].astype(BF) for w in (w_ret_out, w_diff_out, w_o, w_ffn_gate, w_ffn_up, w_ffn_down))
    n1 = norm1_g[layer][None, :]
    n2 = norm2_g[layer][None, :]
    fn = final_norm_g[None, :]
    bg = b_gate[layer][None, :]
    dec_f = ret_decay_fwd[layer][None, :]
    dec_b = ret_decay_bwd[layer][None, :]
    gn = ret_gn_g[layer][None, :]
    lam_params = tuple(p[layer][None, :] for p in (diff_lambda_q1, diff_lambda_k1, diff_lambda_q2, diff_lambda_k2))
    subln = diff_subln_g[layer][None, :]

    xp_flat = x_prompt.reshape(1, batch * seq, D_MODEL)
    assert seq == PROJ_ROWS
    proj_c, kt_new, v_new = _inproj(xp_flat, mod3, n1, w_in_bf, bg, None, latent=False, tm=TOKEN_TILE)
    proj_c_b = proj_c.reshape(batch, seq, W_IN_COLS)
    yr_c, s_f, s_b = _retention(proj_c_b, dec_f, dec_b, gn, None, None, emit_state=True, hb=RET_HEADS)
    od_c = _attention(proj_c_b, None, None, lam_params, subln, bq=seq, hb=DIFF_HEADS)
    y_prompt = _post(xp_flat, yr_c.reshape(1, batch * seq, -1), od_c.reshape(1, batch * seq, -1),
                     proj_c, mod3, n2, fn, post_w, latent=False, tm=TOKEN_TILE).reshape(batch, seq, D_MODEL)

    rope = _rope_tables(dec_seq)
    proj_l = _inproj(x_sample, mod3, n1, w_in_bf, bg, rope, latent=True, tm=TOKEN_TILE)
    yr_l = _retention(proj_l, dec_f, dec_b, gn, state_ret_fwd, state_ret_bwd, emit_state=False, hb=RET_HB_LATENT)
    kt_ctx = jnp.transpose(cache_diff_k[:, layer], (0, 2, 3, 4, 1)).reshape(dec_batch, -1, past_len)
    v_ctx = cache_diff_v[:, layer].reshape(dec_batch, past_len, DIFF_HEADS * 2 * DIFF_DH)
    od_l = _attention(proj_l, kt_ctx, v_ctx, lam_params, subln, bq=ATT_BQ, hb=1)
    y_sample = _post(x_sample, yr_l, od_l, proj_l, mod3, n2, fn, post_w, latent=True, tm=TOKEN_TILE)

    new_diff_k = jnp.transpose(kt_new.reshape(batch, DIFF_HEADS, 2, DIFF_DH, seq), (0, 4, 1, 2, 3))[:, None]
    new_diff_v = v_new.reshape(batch, 1, seq, DIFF_HEADS, 2 * DIFF_DH)
    return (y_prompt, y_sample, s_f, s_b, new_diff_k, new_diff_v)
```

```python
import functools
import math

import jax
import jax.numpy as jnp
import numpy as np
from jax import lax
from jax.experimental import pallas as pl
from jax.experimental.pallas import tpu as pltpu

D_MODEL = 1024
GRID_W = 64
ROPE_BASE = 10000.0
RET_HEADS = 4
RET_DK = 128
RET_DV = 256
RET_CHUNK = 128
DIFF_HEADS = 8
DIFF_DH = 64
FFN_HIDDEN = 2816
W_IN_COLS = 8192
RMS_EPS = 1e-6
GN_EPS = 1e-5
LAM_INIT = 0.8 - 0.6 * math.exp(-0.3 * 0)

LANES = 128
SUBLANES = 8
BF16_SUBLANE_TILE = 16
MXU_DIM = 256
VMEM_LIMIT = 56 * 1024 * 1024

SEG = 1024
MOD_ROWS = SUBLANES
CTX_ROW = 4
K_R_SCALE = RET_DK ** -0.5
Q_D_SCALE = DIFF_DH ** -0.5 * math.log2(math.e)
FFN_SPLITS = ((0, 6 * MXU_DIM), (6 * MXU_DIM, FFN_HIDDEN))

MOD_TN = 2048
TOKEN_TILE = 512
PROJ_ROWS = 256
POST_ROWS = 256
ATT_BQ = 512
ATT_KC = MXU_DIM
ATT_HEAD_STREAMS = 4
ATT_VROWS = LANES + BF16_SUBLANE_TILE
RET_CHAINS = 32
RET_HB_LATENT = 2

BF = jnp.bfloat16
F32 = jnp.float32


def _params(n_axes):
    return pltpu.CompilerParams(dimension_semantics=("arbitrary",) * n_axes,
                                vmem_limit_bytes=VMEM_LIMIT)


def _mod_kernel(c_ref, w_ref, b_ref, o_ref):
    c = c_ref[...]
    s = c * jax.nn.sigmoid(c)
    o_ref[...] = jnp.dot(s, w_ref[...], preferred_element_type=F32) + b_ref[...]


def _modulation(cvec, w_ada, b_ada):
    n_out = w_ada.shape[1]
    tn = MOD_TN
    return pl.pallas_call(
        _mod_kernel,
        out_shape=jax.ShapeDtypeStruct((MOD_ROWS, n_out), F32),
        grid=(n_out // tn,),
        in_specs=[pl.BlockSpec((MOD_ROWS, D_MODEL), lambda n: (0, 0)),
                  pl.BlockSpec((D_MODEL, tn), lambda n: (0, n)),
                  pl.BlockSpec((1, tn), lambda n: (0, n))],
        out_specs=pl.BlockSpec((MOD_ROWS, tn), lambda n: (0, n)),
        compiler_params=_params(1),
        name="modulation",
    )(cvec, w_ada, b_ada)


def _rope_tables(seq_len):
    rows = seq_len // GRID_W
    pos_r = np.repeat(np.arange(rows), GRID_W).astype(np.float64)
    pos_c = np.tile(np.arange(GRID_W), rows).astype(np.float64)

    def angles(d):
        n = d // 4
        inv = ROPE_BASE ** (-np.arange(n, dtype=np.float64) / n)
        return np.concatenate([pos_r[:, None] * inv, pos_c[:, None] * inv], axis=-1)

    ang_r = angles(RET_DK)
    cos_r = np.concatenate([np.cos(ang_r)] * 2, axis=-1)
    sin_r = np.concatenate([-np.sin(ang_r), np.sin(ang_r)], axis=-1)
    ang_d = angles(DIFF_DH)
    zero = np.zeros_like(ang_d)
    cos_d = np.concatenate([np.cos(ang_d)] * 4, axis=-1)
    sin_lo = np.concatenate([zero, np.sin(ang_d)] * 2, axis=-1)
    sin_hi = np.concatenate([-np.sin(ang_d), zero] * 2, axis=-1)
    return tuple(jnp.asarray(t, dtype=F32) for t in (cos_r, sin_r, cos_d, sin_lo, sin_hi))


def _inproj_kernel(*refs, latent, tm):
    if latent:
        (x_ref, sh_ref, sc_ref, g1_ref, w_ref, bg_ref,
         cr_ref, sr_ref, cd_ref, slo_ref, shi_ref, o_ref) = refs
    else:
        x_ref, sh_ref, sc_ref, g1_ref, w_ref, bg_ref, o_ref, ko_ref, vo_ref = refs
    heads = SEG // LANES

    def seg(n):
        return slice(n * SEG, (n + 1) * SEG)

    def normed(rs):
        x = x_ref[0, rs, :]
        ms = jnp.mean(x * x, axis=-1, keepdims=True)
        y = x * lax.rsqrt(ms + RMS_EPS) * g1_ref[...]
        return (y * (1.0 + sc_ref[0]) + sh_ref[0]).astype(BF)

    def rope_r(a, rs):
        return a * cr_ref[rs, :] + pltpu.roll(a, 64, 1) * sr_ref[rs, :]

    def rope_d(a, rs):
        return (a * cd_ref[rs, :] + pltpu.roll(a, 32, 1) * slo_ref[rs, :]
                + pltpu.roll(a, 96, 1) * shi_ref[rs, :])

    def per_head(rs, n, acc, fn):
        for j in range(heads):
            lo = n * SEG + j * LANES
            o_ref[0, rs, lo:lo + LANES] = fn(acc[:, j * LANES:(j + 1) * LANES], j).astype(BF)

    for r in range(tm // PROJ_ROWS):
        rs = slice(r * PROJ_ROWS, (r + 1) * PROJ_ROWS)
        h = normed(rs)

        def project(n):
            return jnp.dot(h, w_ref[:, seg(n)], preferred_element_type=F32)

        def q_or_k_r(a, j):
            a = rope_r(a, rs) if latent else a
            return a * K_R_SCALE if j >= heads // 2 else a

        per_head(rs, 0, project(0), q_or_k_r)
        o_ref[0, rs, seg(1)] = project(1).astype(BF)
        acc = project(2)
        o_ref[0, rs, seg(2)] = (acc * jax.nn.sigmoid(acc)).astype(BF)
        acc = project(3)
        if latent:
            per_head(rs, 3, acc, lambda a, j: rope_d(a, rs) * Q_D_SCALE)
        else:
            o_ref[0, rs, seg(3)] = (acc * Q_D_SCALE).astype(BF)
        acc = project(4)
        if latent:
            per_head(rs, 4, acc, lambda a, j: rope_d(a, rs))
        else:
            o_ref[0, rs, seg(4)] = acc.astype(BF)
            ko_ref[r] = acc.T
        acc = project(5)
        o_ref[0, rs, seg(5)] = acc.astype(BF)
        if not latent:
            vo_ref[0, rs, :] = acc
        for n in (6, 7):
            gate_cols = slice((n - 6) * SEG, (n - 5) * SEG)
            o_ref[0, rs, seg(n)] = jax.nn.sigmoid(project(n) + bg_ref[:, gate_cols]).astype(BF)


def _inproj(x, mod3, norm_g, w_in_bf, b_gate, rope, *, latent, tm):
    bx, lx, _ = x.shape
    row = (lambda b: b) if latent else (lambda b: CTX_ROW)

    def mod_spec(k):
        return pl.BlockSpec((1, 1, D_MODEL), lambda b, m: (row(b), 0, k))

    def resident(shape):
        return pl.BlockSpec(shape, lambda b, m: (0, 0), pipeline_mode=pl.Buffered(1))

    in_specs = [pl.BlockSpec((1, tm, D_MODEL), lambda b, m: (b, m, 0)),
                mod_spec(0), mod_spec(1),
                resident(norm_g.shape), resident(w_in_bf.shape), resident(b_gate.shape)]
    args = [x, mod3, mod3, norm_g, w_in_bf, b_gate]
    proj_shape = jax.ShapeDtypeStruct((bx, lx, W_IN_COLS), BF)
    proj_spec = pl.BlockSpec((1, tm, W_IN_COLS), lambda b, m: (b, m, 0))
    if latent:
        in_specs += [pl.BlockSpec((tm, LANES), lambda b, m: (m, 0))] * 5
        args += list(rope)
        out_shape, out_specs = proj_shape, proj_spec
    else:
        v_shape = jax.ShapeDtypeStruct((bx, lx, SEG), F32)
        v_spec = pl.BlockSpec((1, tm, SEG), lambda b, m: (b, m, 0))
        kt_shape = jax.ShapeDtypeStruct((bx * lx // PROJ_ROWS, SEG, PROJ_ROWS), F32)
        kt_spec = pl.BlockSpec((tm // PROJ_ROWS, SEG, PROJ_ROWS), lambda b, m: (b * (lx // tm) + m, 0, 0))
        out_shape, out_specs = (proj_shape, kt_shape, v_shape), (proj_spec, kt_spec, v_spec)
    return pl.pallas_call(
        functools.partial(_inproj_kernel, latent=latent, tm=tm),
        out_shape=out_shape,
        grid=(bx, lx // tm),
        in_specs=in_specs,
        out_specs=out_specs,
        compiler_params=_params(2),
        name="inproj_latent" if latent else "inproj_ctx",
    )(*args)


def _ret_unroll(hb, n_chunks):
    return max(1, min(RET_CHAINS // hb, n_chunks))


def _ret_kernel(*refs, n_chunks, hb, has_s0, emit_state):
    it = iter(refs)
    q_ref, k_ref, v_ref, g_ref, df_ref, db_ref, gn_ref = (next(it) for _ in range(7))
    s0f_ref, s0b_ref = (next(it), next(it)) if has_s0 else (None, None)
    y_ref = next(it)
    sfo_ref, sbo_ref = (next(it), next(it)) if emit_state else (None, None)
    sf_scr, sb_scr, o_scr = next(it), next(it), next(it)
    C = RET_CHUNK
    unroll = _ret_unroll(hb, n_chunks)
    lane = lax.broadcasted_iota(jnp.int32, (1, RET_HEADS), 1)
    lg_f_all = jax.nn.log_sigmoid(df_ref[...].astype(F32))
    lg_b_all = jax.nn.log_sigmoid(db_ref[...].astype(F32))
    ri = lax.broadcasted_iota(jnp.int32, (C, C), 0).astype(F32)
    ci = lax.broadcasted_iota(jnp.int32, (C, C), 1).astype(F32)
    dist = ri - ci
    rk = lax.broadcasted_iota(jnp.int32, (C, RET_DK), 0).astype(F32)

    def head_consts(j):
        head = pl.program_id(1) * hb + j
        lgf = jnp.sum(jnp.where(lane == head, lg_f_all, 0.0), axis=-1, keepdims=True)
        lgb = jnp.sum(jnp.where(lane == head, lg_b_all, 0.0), axis=-1, keepdims=True)
        decay = jnp.where(dist >= 0, jnp.exp(lgf * jnp.maximum(dist, 0.0)),
                          jnp.exp(lgb * jnp.maximum(-dist, 0.0)))
        return dict(decay=decay,
                    wq_f=jnp.exp(lgf * (rk + 1.0)), wq_b=jnp.exp(lgb * (C - rk)),
                    wk_f=jnp.exp(lgf * (C - 1.0 - rk)), wk_b=jnp.exp(lgb * rk),
                    gc_f=jnp.exp(lgf * C), gc_b=jnp.exp(lgb * C))

    consts = [head_consts(j) for j in range(hb)]

    def rows(c):
        return pl.ds(pl.multiple_of(c * C, C), C)

    def cols(j, width):
        return slice(j * width, (j + 1) * width)

    def run(streams):
        live = list(streams)
        while live:
            live = [g for g in live if next(g, StopIteration) is not StopIteration]

    def kv_chunk(j, c, wk):
        kw = (k_ref[0, rows(c), cols(j, RET_DK)].astype(F32) * wk).astype(BF)
        return lax.dot_general(kw, v_ref[0, rows(c), cols(j, RET_DV)], (((0,), (0,)), ((), ())),
                               preferred_element_type=F32)

    def scan_body(g, carry):
        state = list(carry)
        kvs = {}

        def products(i):
            n = g * unroll + i
            for j in range(hb):
                kvs[i, j] = (kv_chunk(j, n, consts[j]["wk_f"]), kv_chunk(j, n_chunks - 1 - n, consts[j]["wk_b"]))

        def updates(i):
            n = g * unroll + i
            for j in range(hb):
                sf, sb = state[2 * j], state[2 * j + 1]
                sf_scr[j, n] = sf.astype(BF)
                sb_scr[j, n_chunks - 1 - n] = sb.astype(BF)
                state[2 * j] = consts[j]["gc_f"] * sf + kvs[i, j][0]
                state[2 * j + 1] = consts[j]["gc_b"] * sb + kvs[i, j][1]

        products(0)
        for i in range(1, unroll):
            products(i)
            updates(i - 1)
        updates(unroll - 1)
        return tuple(state)

    init = []
    for j in range(hb):
        if has_s0:
            init += [s0f_ref[j].astype(F32), s0b_ref[j].astype(F32)]
        else:
            init += [jnp.zeros((RET_DK, RET_DV), F32)] * 2
    final = lax.fori_loop(0, n_chunks // unroll, scan_body, tuple(init))
    if emit_state:
        for j in range(hb):
            sfo_ref[j] = final[2 * j]
            sbo_ref[j] = final[2 * j + 1]

    def mix_chunk(j, c, buf, idx):
        k = consts[j]
        q = q_ref[0, rows(c), cols(j, RET_DK)]
        qf32 = q.astype(F32)
        v = v_ref[0, rows(c), cols(j, RET_DV)]
        sc = lax.dot_general(q, k_ref[0, rows(c), cols(j, RET_DK)], (((1,), (1,)), ((), ())),
                             preferred_element_type=F32)
        yield
        o = jnp.dot((sc * k["decay"]).astype(BF), v, preferred_element_type=F32)
        o = o + jnp.dot((qf32 * k["wq_f"]).astype(BF), sf_scr[j, c], preferred_element_type=F32)
        o = o + jnp.dot((qf32 * k["wq_b"]).astype(BF), sb_scr[j, c], preferred_element_type=F32)
        o_scr[buf, idx] = o

    def norm_chunk(j, c, buf, idx):
        o = o_scr[buf, idx]
        mu = jnp.mean(o, axis=-1, keepdims=True)
        yield
        oc = o - mu
        var = jnp.mean(oc * oc, axis=-1, keepdims=True)
        yield
        y = oc * lax.rsqrt(var + GN_EPS) * gn_ref[:, cols(j, RET_DV)]
        gate = g_ref[0, rows(c), cols(j, RET_DV)].astype(F32)
        y_ref[0, rows(c), cols(j, RET_DV)] = (gate * y).astype(BF)

    def group(fn, g, buf):
        return [fn(j, g * unroll + i, buf, i * hb + j) for i in range(unroll) for j in range(hb)]

    n_groups = n_chunks // unroll
    run(group(mix_chunk, 0, 0))
    if n_groups > 1:
        assert n_groups % 2 == 0
        run(group(mix_chunk, 1, 1) + group(norm_chunk, 0, 0))

        def out_body(u, _):
            run(group(mix_chunk, 2 * u, 0) + group(norm_chunk, 2 * u - 1, 1))
            run(group(mix_chunk, 2 * u + 1, 1) + group(norm_chunk, 2 * u, 0))
            return 0

        lax.fori_loop(1, n_groups // 2, out_body, 0)
    run(group(norm_chunk, n_groups - 1, (n_groups - 1) % 2))


def _retention(proj, decay_f, decay_b, gn_g, s0_f, s0_b, *, emit_state, hb):
    bx, lx, _ = proj.shape
    n_chunks = lx // RET_CHUNK
    has_s0 = s0_f is not None
    groups = RET_HEADS // hb
    k_off = (RET_HEADS * RET_DK) // (hb * RET_DK)
    v_off = (2 * RET_HEADS * RET_DK) // (hb * RET_DV)
    g_off = v_off + groups
    state_spec = pl.BlockSpec((None, None, hb, RET_DK, RET_DV), lambda b, h: (b, 0, h, 0, 0))
    in_specs = [pl.BlockSpec((1, lx, hb * RET_DK), lambda b, h: (b, 0, h)),
                pl.BlockSpec((1, lx, hb * RET_DK), lambda b, h: (b, 0, k_off + h)),
                pl.BlockSpec((1, lx, hb * RET_DV), lambda b, h: (b, 0, v_off + h)),
                pl.BlockSpec((1, lx, hb * RET_DV), lambda b, h: (b, 0, g_off + h)),
                pl.BlockSpec((1, RET_HEADS), lambda b, h: (0, 0)),
                pl.BlockSpec((1, RET_HEADS), lambda b, h: (0, 0)),
                pl.BlockSpec((1, hb * RET_DV), lambda b, h: (0, h))]
    args = [proj, proj, proj, proj, decay_f, decay_b, gn_g]
    if has_s0:
        in_specs += [state_spec, state_spec]
        args += [s0_f, s0_b]
    y_shape = jax.ShapeDtypeStruct((bx, lx, RET_HEADS * RET_DV), BF)
    y_spec = pl.BlockSpec((1, lx, hb * RET_DV), lambda b, h: (b, 0, h))
    if emit_state:
        st_shape = jax.ShapeDtypeStruct((bx, 1, RET_HEADS, RET_DK, RET_DV), F32)
        out_shape, out_specs = (y_shape, st_shape, st_shape), (y_spec, state_spec, state_spec)
    else:
        out_shape, out_specs = y_shape, y_spec
    snap = pltpu.VMEM((hb, n_chunks, RET_DK, RET_DV), BF)
    return pl.pallas_call(
        functools.partial(_ret_kernel, n_chunks=n_chunks, hb=hb, has_s0=has_s0, emit_state=emit_state),
        out_shape=out_shape,
        grid=(bx, groups),
        in_specs=in_specs,
        out_specs=out_specs,
        scratch_shapes=[snap, snap,
                        pltpu.VMEM((2, _ret_unroll(hb, n_chunks) * hb, RET_CHUNK, RET_DV), F32)],
        compiler_params=_params(2),
        name="retention_latent" if has_s0 else "retention_ctx",
    )(*args)


def _attn_kernel(*refs, lx, lk_ctx, bq, hb, carry):
    it = iter(refs)
    q_ref, k_ref, v_ref = next(it), next(it), next(it)
    kctx_ref, vctx_ref = (next(it), next(it)) if lk_ctx else (None, None)
    qn_ref, kn_ref, kctxn_ref = (next(it), next(it), next(it)) if carry else (None, None, None)
    lq1_ref, lk1_ref, lq2_ref, lk2_ref, sg_ref = (next(it) for _ in range(5))
    o_ref = next(it)
    vt, s_scr, m_scr, acc_scr = next(it), next(it), next(it), next(it)
    kctx_scr = next(it) if lk_ctx else None
    lk = lx + lk_ctx
    n_kc = lk // ATT_KC
    n_q = lx // bq

    def cols(j):
        return slice(j * LANES, (j + 1) * LANES)

    def chunk(c):
        return slice(c * ATT_KC, (c + 1) * ATT_KC)

    def keys(k_src, kctx_slot, j, c):
        lo = c * ATT_KC
        if lo < lx:
            return k_src[0, lo:lo + ATT_KC, cols(j)]
        return kctx_scr[kctx_slot, lo - lx:lo - lx + ATT_KC, :]

    if carry:
        kctx_scr[hb] = kctxn_ref[0, cols(0), :].T.astype(BF)
    for j in range(hb):
        if lk_ctx:
            kctx_scr[j] = kctx_ref[0, cols(j), :].T.astype(BF)
        for c in range(n_kc):
            lo = c * ATT_KC
            if lo < lx:
                vc = v_ref[0, lo:lo + ATT_KC, cols(j)].astype(F32)
            else:
                vc = vctx_ref[0, lo - lx:lo - lx + ATT_KC, cols(j)]
            vt[j, 0:LANES, chunk(c)] = vc.T.astype(BF)
        vt[j, LANES:ATT_VROWS, :] = jnp.ones((ATT_VROWS - LANES, lk), BF)

    def lam_term(a_ref, b_ref):
        return jnp.exp(jnp.sum(a_ref[...].astype(F32) * b_ref[...].astype(F32), axis=-1, keepdims=True))

    lam = lam_term(lq1_ref, lk1_ref) - lam_term(lq2_ref, lk2_ref) + LAM_INIT
    lane = lax.broadcasted_iota(jnp.int32, (bq, LANES), 1)

    def rows(t):
        return pl.ds(pl.multiple_of(t * bq, bq), bq)

    def scores(j, t, slot, following=False):
        q_src, k_src, kctx_slot = (qn_ref, kn_ref, hb) if following else (q_ref, k_ref, j)
        q = q_src[0, rows(t), cols(j)]
        zero = jnp.zeros_like(q)
        q_maps = (jnp.where(lane < DIFF_DH, q, zero), jnp.where(lane >= DIFF_DH, q, zero))
        m = [jnp.full((SUBLANES, bq), -jnp.inf, F32) for _ in range(2)]
        for c in range(n_kc):
            kk = keys(k_src, kctx_slot, j, c)
            for i in range(2):
                s = lax.dot_general(kk, q_maps[i], (((1,), (1,)), ((), ())),
                                    preferred_element_type=F32)
                s_scr[j, slot, i, chunk(c), :] = s
                m[i] = jnp.maximum(m[i], jnp.max(s.reshape(ATT_KC // SUBLANES, SUBLANES, bq), axis=0))
            yield
        for i in range(2):
            m_scr[j, slot, i] = jnp.broadcast_to(jnp.max(m[i], axis=0, keepdims=True), (SUBLANES, bq))

    def values(j, slot):
        m = [m_scr[j, slot, i][0:1, :] for i in range(2)]
        acc = [jnp.zeros((ATT_VROWS, bq), F32) for _ in range(2)]
        for c in range(n_kc):
            for i in range(2):
                e = jnp.exp2(s_scr[j, slot, i, chunk(c), :] - m[i]).astype(BF)
                acc[i] = acc[i] + jnp.dot(vt[j, :, chunk(c)], e, preferred_element_type=F32)
            yield
        for i in range(2):
            acc_scr[j, slot, i] = acc[i]

    def finish(j, t, slot):
        outs = [acc_scr[j, slot, i, 0:LANES, :] / acc_scr[j, slot, i, LANES:LANES + 1, :] for i in range(2)]
        ot = outs[0] - lam * outs[1]
        ms = jnp.mean(ot * ot, axis=0, keepdims=True)
        o = (ot * lax.rsqrt(ms + RMS_EPS)).T * sg_ref[...]
        o_ref[0, rows(t), cols(j)] = (o * (1.0 - LAM_INIT)).astype(BF)
        yield

    def chain(*streams):
        for g in streams:
            yield from g

    def run(*streams):
        live = list(streams)
        while live:
            live = [g for g in live if next(g, StopIteration) is not StopIteration]

    if n_q == 1:
        for j0 in range(0, hb, ATT_HEAD_STREAMS):
            run(*[chain(scores(j, 0, 0), values(j, 0), finish(j, 0, 0))
                  for j in range(j0, min(j0 + ATT_HEAD_STREAMS, hb))])
        return
    assert hb == 1 and n_q % 2 == 0 and n_q >= 4 and carry
    @pl.when((pl.program_id(0) == 0) & (pl.program_id(1) == 0))
    def _():
        run(scores(0, 0, 0))

    run(scores(0, 1, 1), values(0, 0))
    run(scores(0, 2, 0), chain(finish(0, 0, 0), values(0, 1)))

    def body(u, _):
        t = 2 * u
        run(scores(0, t + 1, 1), chain(finish(0, t - 1, 1), values(0, 0)))
        run(scores(0, t + 2, 0), chain(finish(0, t, 0), values(0, 1)))
        return 0

    lax.fori_loop(1, (n_q - 2) // 2, body, 0)
    run(scores(0, n_q - 1, 1), chain(finish(0, n_q - 3, 1), values(0, 0)))
    run(scores(0, 0, 0, following=True), chain(finish(0, n_q - 2, 0), values(0, 1)))
    run(finish(0, n_q - 1, 1))


def _attention(proj, kt_ctx, v_ctx, lam_params, subln_g, *, bq, hb):
    bx, lx, _ = proj.shape
    lk_ctx = 0 if kt_ctx is None else kt_ctx.shape[2]
    lk = lx + lk_ctx
    head_w = 2 * DIFF_DH
    width = hb * head_w
    q_off = 3 * SEG // width
    k_off = 4 * SEG // width
    v_off = 5 * SEG // width

    def heads(off):
        return pl.BlockSpec((1, lx, width), lambda b, h: (b, 0, off + h))

    in_specs = [heads(q_off), heads(k_off), heads(v_off)]
    args = [proj, proj, proj]
    scratch = [pltpu.VMEM((hb, ATT_VROWS, lk), BF),
               pltpu.VMEM((hb, 2, 2, lk, bq), F32), pltpu.VMEM((hb, 2, 2, SUBLANES, bq), F32),
               pltpu.VMEM((hb, 2, 2, ATT_VROWS, bq), F32)]
    carry = lx // bq > 1
    if lk_ctx:
        assert lk_ctx % ATT_KC == 0
        in_specs += [pl.BlockSpec((1, width, lk_ctx), lambda b, h: (b, h, 0)),
                     pl.BlockSpec((1, lk_ctx, width), lambda b, h: (b, 0, h))]
        args += [kt_ctx, v_ctx]
        scratch.append(pltpu.VMEM((hb + carry, lk_ctx, head_w), BF))
    if carry:
        assert hb == 1 and lk_ctx
        last = bx * DIFF_HEADS - 1

        def following(b, h):
            g = jnp.minimum(b * DIFF_HEADS + h + 1, last)
            return g // DIFF_HEADS, g % DIFF_HEADS

        in_specs += [pl.BlockSpec((1, lx, width), lambda b, h: (following(b, h)[0], 0, q_off + following(b, h)[1])),
                     pl.BlockSpec((1, lx, width), lambda b, h: (following(b, h)[0], 0, k_off + following(b, h)[1])),
                     pl.BlockSpec((1, width, lk_ctx), lambda b, h: (following(b, h)[0], following(b, h)[1], 0))]
        args += [proj, proj, kt_ctx]
    small = pl.BlockSpec((1, DIFF_DH), lambda b, h: (0, 0))
    in_specs += [small] * 4 + [pl.BlockSpec((1, head_w), lambda b, h: (0, 0))]
    args += list(lam_params) + [subln_g]
    return pl.pallas_call(
        functools.partial(_attn_kernel, lx=lx, lk_ctx=lk_ctx, bq=bq, hb=hb, carry=carry),
        out_shape=jax.ShapeDtypeStruct((bx, lx, DIFF_HEADS * head_w), BF),
        grid=(bx, DIFF_HEADS // hb),
        in_specs=in_specs,
        out_specs=heads(0),
        scratch_shapes=scratch,
        compiler_params=_params(2),
        name="diff_attn_latent" if lk_ctx else "diff_attn_ctx",
    )(*args)


def _post_kernel(x_ref, yr_ref, od_ref, gr_ref, gd_ref, gate1_ref, sh2_ref, sc2_ref, gate2_ref,
                 n2_ref, fn_ref, wr_ref, wd_ref, wo_ref, wg_ref, wu_ref, wdn_ref, o_ref, *, tm):
    def rows_stream(rs):
        br = jnp.dot(yr_ref[0, rs, :], wr_ref[...], preferred_element_type=F32)
        bd = jnp.dot(od_ref[0, rs, :], wd_ref[...], preferred_element_type=F32)
        yield
        merged = (gr_ref[0, rs, :].astype(F32) * br + gd_ref[0, rs, :].astype(F32) * bd).astype(BF)
        mix = jnp.dot(merged, wo_ref[...], preferred_element_type=F32)
        yield
        x1 = x_ref[0, rs, :] + gate1_ref[0] * mix
        ms = jnp.mean(x1 * x1, axis=-1, keepdims=True)
        h2 = x1 * lax.rsqrt(ms + RMS_EPS) * n2_ref[...]
        h2 = (h2 * (1.0 + sc2_ref[0]) + sh2_ref[0]).astype(BF)
        ffn = None
        for lo, hi in FFN_SPLITS:
            g = jnp.dot(h2, wg_ref[:, lo:hi], preferred_element_type=F32)
            u = jnp.dot(h2, wu_ref[:, lo:hi], preferred_element_type=F32)
            yield
            act = (g * jax.nn.sigmoid(g) * u).astype(BF)
            part = jnp.dot(act, wdn_ref[lo:hi, :], preferred_element_type=F32)
            ffn = part if ffn is None else ffn + part
            yield
        x2 = x1 + gate2_ref[0] * ffn
        ms2 = jnp.mean(x2 * x2, axis=-1, keepdims=True)
        o_ref[0, rs, :] = x2 * lax.rsqrt(ms2 + RMS_EPS) * fn_ref[...]

    def delayed(stream, stages):
        for _ in range(stages):
            yield
        yield from stream

    n_groups = tm // POST_ROWS
    live = [delayed(rows_stream(slice(r * POST_ROWS, (r + 1) * POST_ROWS)), r) for r in range(n_groups)]
    while live:
        live = [g for g in live if next(g, StopIteration) is not StopIteration]


def _post(x, y_r, o_d, proj, mod3, norm2_g, final_g, weights, *, latent, tm):
    bx, lx, _ = x.shape
    row = (lambda b: b) if latent else (lambda b: CTX_ROW)

    def tok(width, col):
        return pl.BlockSpec((1, tm, width), lambda b, m: (b, m, col))

    def mod_spec(k):
        return pl.BlockSpec((1, 1, D_MODEL), lambda b, m: (row(b), 0, k))

    def resident(shape):
        return pl.BlockSpec(shape, lambda b, m: (0, 0), pipeline_mode=pl.Buffered(1))

    vec = pl.BlockSpec((1, D_MODEL), lambda b, m: (0, 0))
    in_specs = [tok(D_MODEL, 0), tok(D_MODEL, 0), tok(D_MODEL, 0), tok(SEG, 6), tok(SEG, 7),
                mod_spec(2), mod_spec(3), mod_spec(4), mod_spec(5), vec, vec]
    in_specs += [resident(w.shape) for w in weights]
    return pl.pallas_call(
        functools.partial(_post_kernel, tm=tm),
        out_shape=jax.ShapeDtypeStruct((bx, lx, D_MODEL), F32),
        grid=(bx, lx // tm),
        in_specs=in_specs,
        out_specs=tok(D_MODEL, 0),
        compiler_params=_params(2),
        name="post_latent" if latent else "post_ctx",
    )(x, y_r, o_d, proj, proj, mod3, mod3, mod3, mod3, norm2_g, final_g, *weights)


def kernel(x_prompt, x_sample, state_ret_fwd, state_ret_bwd, cache_diff_k, cache_diff_v, c, c_ctx,
           norm1_g, norm2_g, w_ada, b_ada, w_in, b_gate, ret_decay_fwd, ret_decay_bwd, ret_gn_g,
           w_ret_out, diff_lambda_q1, diff_lambda_k1, diff_lambda_q2, diff_lambda_k2, diff_subln_g,
           w_diff_out, w_o, w_ffn_gate, w_ffn_up, w_ffn_down, final_norm_g):
    batch, seq, _ = x_prompt.shape
    dec_batch, dec_seq, _ = x_sample.shape
    past_len = cache_diff_k.shape[2]
    layer = 0

    cvec = jnp.zeros((MOD_ROWS, D_MODEL), F32).at[:dec_batch].set(c).at[CTX_ROW].set(c_ctx)
    mod = _modulation(cvec, w_ada[layer], b_ada[layer][None, :])
    mod3 = mod.reshape(MOD_ROWS, 1, 6 * D_MODEL)

    w_in_bf = w_in[layer].astype(BF)
    post_w = tuple(w[layer].astype(BF) for w in (w_ret_out, w_diff_out, w_o, w_ffn_gate, w_ffn_up, w_ffn_down))
    n1 = norm1_g[layer][None, :]
    n2 = norm2_g[layer][None, :]
    fn = final_norm_g[None, :]
    bg = b_gate[layer][None, :]
    dec_f = ret_decay_fwd[layer][None, :]
    dec_b = ret_decay_bwd[layer][None, :]
    gn = ret_gn_g[layer][None, :]
    lam_params = tuple(p[layer][None, :] for p in (diff_lambda_q1, diff_lambda_k1, diff_lambda_q2, diff_lambda_k2))
    subln = diff_subln_g[layer][None, :]

    xp_flat = x_prompt.reshape(1, batch * seq, D_MODEL)
    assert seq == PROJ_ROWS
    proj_c, kt_new, v_new = _inproj(xp_flat, mod3, n1, w_in_bf, bg, None, latent=False, tm=TOKEN_TILE)
    proj_c_b = proj_c.reshape(batch, seq, W_IN_COLS)
    yr_c, s_f, s_b = _retention(proj_c_b, dec_f, dec_b, gn, None, None, emit_state=True, hb=RET_HEADS)
    od_c = _attention(proj_c_b, None, None, lam_params, subln, bq=seq, hb=DIFF_HEADS)
    y_prompt = _post(xp_flat, yr_c.reshape(1, batch * seq, -1), od_c.reshape(1, batch * seq, -1),
                     proj_c, mod3, n2, fn, post_w, latent=False, tm=TOKEN_TILE).reshape(batch, seq, D_MODEL)

    rope = _rope_tables(dec_seq)
    proj_l = _inproj(x_sample, mod3, n1, w_in_bf, bg, rope, latent=True, tm=TOKEN_TILE)
    yr_l = _retention(proj_l, dec_f, dec_b, gn, state_ret_fwd, state_ret_bwd, emit_state=False, hb=RET_HB_LATENT)
    kt_ctx = jnp.transpose(cache_diff_k[:, layer], (0, 2, 3, 4, 1)).reshape(dec_batch, -1, past_len)
    v_ctx = cache_diff_v[:, layer].reshape(dec_batch, past_len, DIFF_HEADS * 2 * DIFF_DH)
    od_l = _attention(proj_l, kt_ctx, v_ctx, lam_params, subln, bq=ATT_BQ, hb=1)
    y_sample = _post(x_sample, yr_l, od_l, proj_l, mod3, n2, fn, post_w, latent=True, tm=TOKEN_TILE)

    new_diff_k = jnp.transpose(kt_new.reshape(batch, DIFF_HEADS, 2, DIFF_DH, seq), (0, 4, 1, 2, 3))[:, None]
    new_diff_v = v_new.reshape(batch, 1, seq, DIFF_HEADS, 2 * DIFF_DH)
    return (y_prompt, y_sample, s_f, s_b, new_diff_k, new_diff_v)
```

```python
import functools
import math

import jax
import jax.numpy as jnp
import numpy as np
from jax import lax
from jax.experimental import pallas as pl
from jax.experimental.pallas import tpu as pltpu

D_MODEL = 1024
GRID_W = 64
ROPE_BASE = 10000.0
RET_HEADS = 4
RET_DK = 128
RET_DV = 256
RET_CHUNK = 128
DIFF_HEADS = 8
DIFF_DH = 64
FFN_HIDDEN = 2816
W_IN_COLS = 8192
RMS_EPS = 1e-6
GN_EPS = 1e-5
LAM_INIT = 0.8 - 0.6 * math.exp(-0.3 * 0)

LANES = 128
SUBLANES = 8
BF16_SUBLANE_TILE = 16
MXU_DIM = 256
VMEM_LIMIT = 56 * 1024 * 1024

SEG = 1024
MIXER_COLS = 6 * SEG
GATE_COLS = 2 * SEG
MOD_ROWS = SUBLANES
CTX_ROW = 4
K_R_SCALE = RET_DK ** -0.5
Q_D_SCALE = DIFF_DH ** -0.5 * math.log2(math.e)
FFN_SPLITS = ((0, 6 * MXU_DIM), (6 * MXU_DIM, FFN_HIDDEN))

MOD_TN = 2048
TOKEN_TILE = 512
PROJ_ROWS = 256
POST_ROWS = 256
ATT_BQ = 512
ATT_KC = MXU_DIM
ATT_HEAD_STREAMS = 4
ATT_VROWS = LANES + BF16_SUBLANE_TILE
RET_CHAINS = 32
RET_HB_LATENT = 2

BF = jnp.bfloat16
F32 = jnp.float32


def _params(n_axes):
    return pltpu.CompilerParams(dimension_semantics=("arbitrary",) * n_axes,
                                vmem_limit_bytes=VMEM_LIMIT)


def _mod_kernel(c_ref, w_ref, b_ref, o_ref):
    c = c_ref[...]
    s = c * jax.nn.sigmoid(c)
    o_ref[...] = jnp.dot(s, w_ref[...], preferred_element_type=F32) + b_ref[...]


def _modulation(cvec, w_ada, b_ada):
    n_out = w_ada.shape[1]
    tn = MOD_TN
    return pl.pallas_call(
        _mod_kernel,
        out_shape=jax.ShapeDtypeStruct((MOD_ROWS, n_out), F32),
        grid=(n_out // tn,),
        in_specs=[pl.BlockSpec((MOD_ROWS, D_MODEL), lambda n: (0, 0)),
                  pl.BlockSpec((D_MODEL, tn), lambda n: (0, n)),
                  pl.BlockSpec((1, tn), lambda n: (0, n))],
        out_specs=pl.BlockSpec((MOD_ROWS, tn), lambda n: (0, n)),
        compiler_params=_params(1),
        name="modulation",
    )(cvec, w_ada, b_ada)


def _rope_tables(seq_len):
    rows = seq_len // GRID_W
    pos_r = np.repeat(np.arange(rows), GRID_W).astype(np.float64)
    pos_c = np.tile(np.arange(GRID_W), rows).astype(np.float64)

    def angles(d):
        n = d // 4
        inv = ROPE_BASE ** (-np.arange(n, dtype=np.float64) / n)
        return np.concatenate([pos_r[:, None] * inv, pos_c[:, None] * inv], axis=-1)

    ang_r = angles(RET_DK)
    cos_r = np.concatenate([np.cos(ang_r)] * 2, axis=-1)
    sin_r = np.concatenate([-np.sin(ang_r), np.sin(ang_r)], axis=-1)
    ang_d = angles(DIFF_DH)
    zero = np.zeros_like(ang_d)
    cos_d = np.concatenate([np.cos(ang_d)] * 4, axis=-1)
    sin_lo = np.concatenate([zero, np.sin(ang_d)] * 2, axis=-1)
    sin_hi = np.concatenate([-np.sin(ang_d), zero] * 2, axis=-1)
    return tuple(jnp.asarray(t, dtype=F32) for t in (cos_r, sin_r, cos_d, sin_lo, sin_hi))


def _inproj_kernel(*refs, latent, tm):
    if latent:
        (x_ref, sh_ref, sc_ref, g1_ref, w_ref,
         cr_ref, sr_ref, cd_ref, slo_ref, shi_ref, o_ref) = refs
    else:
        x_ref, sh_ref, sc_ref, g1_ref, w_ref, o_ref, ko_ref, vo_ref = refs
    heads = SEG // LANES

    def seg(n):
        return slice(n * SEG, (n + 1) * SEG)

    def normed(rs):
        x = x_ref[0, rs, :]
        ms = jnp.mean(x * x, axis=-1, keepdims=True)
        y = x * lax.rsqrt(ms + RMS_EPS) * g1_ref[...]
        return (y * (1.0 + sc_ref[0]) + sh_ref[0]).astype(BF)

    def rope_r(a, rs):
        return a * cr_ref[rs, :] + pltpu.roll(a, 64, 1) * sr_ref[rs, :]

    def rope_d(a, rs):
        return (a * cd_ref[rs, :] + pltpu.roll(a, 32, 1) * slo_ref[rs, :]
                + pltpu.roll(a, 96, 1) * shi_ref[rs, :])

    def per_head(rs, n, acc, fn):
        for j in range(heads):
            lo = n * SEG + j * LANES
            o_ref[0, rs, lo:lo + LANES] = fn(acc[:, j * LANES:(j + 1) * LANES], j).astype(BF)

    for r in range(tm // PROJ_ROWS):
        rs = slice(r * PROJ_ROWS, (r + 1) * PROJ_ROWS)
        h = normed(rs)

        def project(n):
            return jnp.dot(h, w_ref[:, seg(n)], preferred_element_type=F32)

        def q_or_k_r(a, j):
            a = rope_r(a, rs) if latent else a
            return a * K_R_SCALE if j >= heads // 2 else a

        per_head(rs, 0, project(0), q_or_k_r)
        o_ref[0, rs, seg(1)] = project(1).astype(BF)
        acc = project(2)
        o_ref[0, rs, seg(2)] = (acc * jax.nn.sigmoid(acc)).astype(BF)
        acc = project(3)
        if latent:
            per_head(rs, 3, acc, lambda a, j: rope_d(a, rs) * Q_D_SCALE)
        else:
            o_ref[0, rs, seg(3)] = (acc * Q_D_SCALE).astype(BF)
        acc = project(4)
        if latent:
            per_head(rs, 4, acc, lambda a, j: rope_d(a, rs))
        else:
            o_ref[0, rs, seg(4)] = acc.astype(BF)
            ko_ref[r] = acc.T
        acc = project(5)
        o_ref[0, rs, seg(5)] = acc.astype(BF)
        if not latent:
            vo_ref[0, rs, :] = acc


def _inproj(x, mod3, norm_g, w_in_bf, rope, *, latent, tm):
    bx, lx, _ = x.shape
    row = (lambda b: b) if latent else (lambda b: CTX_ROW)

    def mod_spec(k):
        return pl.BlockSpec((1, 1, D_MODEL), lambda b, m: (row(b), 0, k))

    def resident(shape):
        return pl.BlockSpec(shape, lambda b, m: (0, 0), pipeline_mode=pl.Buffered(1))

    in_specs = [pl.BlockSpec((1, tm, D_MODEL), lambda b, m: (b, m, 0)),
                mod_spec(0), mod_spec(1),
                resident(norm_g.shape), resident((D_MODEL, MIXER_COLS))]
    args = [x, mod3, mod3, norm_g, w_in_bf]
    proj_shape = jax.ShapeDtypeStruct((bx, lx, MIXER_COLS), BF)
    proj_spec = pl.BlockSpec((1, tm, MIXER_COLS), lambda b, m: (b, m, 0))
    if latent:
        in_specs += [pl.BlockSpec((tm, LANES), lambda b, m: (m, 0))] * 5
        args += list(rope)
        out_shape, out_specs = proj_shape, proj_spec
    else:
        v_shape = jax.ShapeDtypeStruct((bx, lx, SEG), F32)
        v_spec = pl.BlockSpec((1, tm, SEG), lambda b, m: (b, m, 0))
        kt_shape = jax.ShapeDtypeStruct((bx * lx // PROJ_ROWS, SEG, PROJ_ROWS), F32)
        kt_spec = pl.BlockSpec((tm // PROJ_ROWS, SEG, PROJ_ROWS), lambda b, m: (b * (lx // tm) + m, 0, 0))
        out_shape, out_specs = (proj_shape, kt_shape, v_shape), (proj_spec, kt_spec, v_spec)
    return pl.pallas_call(
        functools.partial(_inproj_kernel, latent=latent, tm=tm),
        out_shape=out_shape,
        grid=(bx, lx // tm),
        in_specs=in_specs,
        out_specs=out_specs,
        compiler_params=_params(2),
        name="inproj_latent" if latent else "inproj_ctx",
    )(*args)


def _ret_unroll(hb, n_chunks):
    return max(1, min(RET_CHAINS // hb, n_chunks))


def _ret_kernel(*refs, n_chunks, hb, has_s0, emit_state):
    it = iter(refs)
    q_ref, k_ref, v_ref, g_ref, df_ref, db_ref, gn_ref = (next(it) for _ in range(7))
    s0f_ref, s0b_ref = (next(it), next(it)) if has_s0 else (None, None)
    y_ref = next(it)
    sfo_ref, sbo_ref = (next(it), next(it)) if emit_state else (None, None)
    sf_scr, sb_scr, o_scr = next(it), next(it), next(it)
    C = RET_CHUNK
    unroll = _ret_unroll(hb, n_chunks)
    lane = lax.broadcasted_iota(jnp.int32, (1, RET_HEADS), 1)
    lg_f_all = jax.nn.log_sigmoid(df_ref[...].astype(F32))
    lg_b_all = jax.nn.log_sigmoid(db_ref[...].astype(F32))
    ri = lax.broadcasted_iota(jnp.int32, (C, C), 0).astype(F32)
    ci = lax.broadcasted_iota(jnp.int32, (C, C), 1).astype(F32)
    dist = ri - ci
    rk = lax.broadcasted_iota(jnp.int32, (C, RET_DK), 0).astype(F32)

    def head_consts(j):
        head = pl.program_id(1) * hb + j
        lgf = jnp.sum(jnp.where(lane == head, lg_f_all, 0.0), axis=-1, keepdims=True)
        lgb = jnp.sum(jnp.where(lane == head, lg_b_all, 0.0), axis=-1, keepdims=True)
        decay = jnp.where(dist >= 0, jnp.exp(lgf * jnp.maximum(dist, 0.0)),
                          jnp.exp(lgb * jnp.maximum(-dist, 0.0)))
        return dict(decay=decay,
                    wq_f=jnp.exp(lgf * (rk + 1.0)), wq_b=jnp.exp(lgb * (C - rk)),
                    wk_f=jnp.exp(lgf * (C - 1.0 - rk)), wk_b=jnp.exp(lgb * rk),
                    gc_f=jnp.exp(lgf * C), gc_b=jnp.exp(lgb * C))

    consts = [head_consts(j) for j in range(hb)]

    def rows(c):
        return pl.ds(pl.multiple_of(c * C, C), C)

    def cols(j, width):
        return slice(j * width, (j + 1) * width)

    def run(streams):
        live = list(streams)
        while live:
            live = [g for g in live if next(g, StopIteration) is not StopIteration]

    def kv_chunk(j, c, wk):
        kw = (k_ref[0, rows(c), cols(j, RET_DK)].astype(F32) * wk).astype(BF)
        return lax.dot_general(kw, v_ref[0, rows(c), cols(j, RET_DV)], (((0,), (0,)), ((), ())),
                               preferred_element_type=F32)

    def scan_body(g, carry):
        state = list(carry)
        kvs = {}

        def products(i):
            n = g * unroll + i
            for j in range(hb):
                kvs[i, j] = (kv_chunk(j, n, consts[j]["wk_f"]), kv_chunk(j, n_chunks - 1 - n, consts[j]["wk_b"]))

        def updates(i):
            n = g * unroll + i
            for j in range(hb):
                sf, sb = state[2 * j], state[2 * j + 1]
                sf_scr[j, n] = sf.astype(BF)
                sb_scr[j, n_chunks - 1 - n] = sb.astype(BF)
                state[2 * j] = consts[j]["gc_f"] * sf + kvs[i, j][0]
                state[2 * j + 1] = consts[j]["gc_b"] * sb + kvs[i, j][1]

        products(0)
        for i in range(1, unroll):
            products(i)
            updates(i - 1)
        updates(unroll - 1)
        return tuple(state)

    init = []
    for j in range(hb):
        if has_s0:
            init += [s0f_ref[j].astype(F32), s0b_ref[j].astype(F32)]
        else:
            init += [jnp.zeros((RET_DK, RET_DV), F32)] * 2
    final = lax.fori_loop(0, n_chunks // unroll, scan_body, tuple(init))
    if emit_state:
        for j in range(hb):
            sfo_ref[j] = final[2 * j]
            sbo_ref[j] = final[2 * j + 1]

    def mix_chunk(j, c, buf, idx):
        k = consts[j]
        q = q_ref[0, rows(c), cols(j, RET_DK)]
        qf32 = q.astype(F32)
        v = v_ref[0, rows(c), cols(j, RET_DV)]
        sc = lax.dot_general(q, k_ref[0, rows(c), cols(j, RET_DK)], (((1,), (1,)), ((), ())),
                             preferred_element_type=F32)
        yield
        o = jnp.dot((sc * k["decay"]).astype(BF), v, preferred_element_type=F32)
        o = o + jnp.dot((qf32 * k["wq_f"]).astype(BF), sf_scr[j, c], preferred_element_type=F32)
        o = o + jnp.dot((qf32 * k["wq_b"]).astype(BF), sb_scr[j, c], preferred_element_type=F32)
        o_scr[buf, idx] = o

    def norm_chunk(j, c, buf, idx):
        o = o_scr[buf, idx]
        mu = jnp.mean(o, axis=-1, keepdims=True)
        yield
        oc = o - mu
        var = jnp.mean(oc * oc, axis=-1, keepdims=True)
        yield
        y = oc * lax.rsqrt(var + GN_EPS) * gn_ref[:, cols(j, RET_DV)]
        gate = g_ref[0, rows(c), cols(j, RET_DV)].astype(F32)
        y_ref[0, rows(c), cols(j, RET_DV)] = (gate * y).astype(BF)

    def group(fn, g, buf):
        return [fn(j, g * unroll + i, buf, i * hb + j) for i in range(unroll) for j in range(hb)]

    n_groups = n_chunks // unroll
    run(group(mix_chunk, 0, 0))
    if n_groups > 1:
        assert n_groups % 2 == 0
        run(group(mix_chunk, 1, 1) + group(norm_chunk, 0, 0))

        def out_body(u, _):
            run(group(mix_chunk, 2 * u, 0) + group(norm_chunk, 2 * u - 1, 1))
            run(group(mix_chunk, 2 * u + 1, 1) + group(norm_chunk, 2 * u, 0))
            return 0

        lax.fori_loop(1, n_groups // 2, out_body, 0)
    run(group(norm_chunk, n_groups - 1, (n_groups - 1) % 2))


def _retention(proj, decay_f, decay_b, gn_g, s0_f, s0_b, *, emit_state, hb):
    bx, lx, _ = proj.shape
    n_chunks = lx // RET_CHUNK
    has_s0 = s0_f is not None
    groups = RET_HEADS // hb
    k_off = (RET_HEADS * RET_DK) // (hb * RET_DK)
    v_off = (2 * RET_HEADS * RET_DK) // (hb * RET_DV)
    g_off = v_off + groups
    state_spec = pl.BlockSpec((None, None, hb, RET_DK, RET_DV), lambda b, h: (b, 0, h, 0, 0))
    in_specs = [pl.BlockSpec((1, lx, hb * RET_DK), lambda b, h: (b, 0, h)),
                pl.BlockSpec((1, lx, hb * RET_DK), lambda b, h: (b, 0, k_off + h)),
                pl.BlockSpec((1, lx, hb * RET_DV), lambda b, h: (b, 0, v_off + h)),
                pl.BlockSpec((1, lx, hb * RET_DV), lambda b, h: (b, 0, g_off + h)),
                pl.BlockSpec((1, RET_HEADS), lambda b, h: (0, 0)),
                pl.BlockSpec((1, RET_HEADS), lambda b, h: (0, 0)),
                pl.BlockSpec((1, hb * RET_DV), lambda b, h: (0, h))]
    args = [proj, proj, proj, proj, decay_f, decay_b, gn_g]
    if has_s0:
        in_specs += [state_spec, state_spec]
        args += [s0_f, s0_b]
    y_shape = jax.ShapeDtypeStruct((bx, lx, RET_HEADS * RET_DV), BF)
    y_spec = pl.BlockSpec((1, lx, hb * RET_DV), lambda b, h: (b, 0, h))
    if emit_state:
        st_shape = jax.ShapeDtypeStruct((bx, 1, RET_HEADS, RET_DK, RET_DV), F32)
        out_shape, out_specs = (y_shape, st_shape, st_shape), (y_spec, state_spec, state_spec)
    else:
        out_shape, out_specs = y_shape, y_spec
    snap = pltpu.VMEM((hb, n_chunks, RET_DK, RET_DV), BF)
    return pl.pallas_call(
        functools.partial(_ret_kernel, n_chunks=n_chunks, hb=hb, has_s0=has_s0, emit_state=emit_state),
        out_shape=out_shape,
        grid=(bx, groups),
        in_specs=in_specs,
        out_specs=out_specs,
        scratch_shapes=[snap, snap,
                        pltpu.VMEM((2, _ret_unroll(hb, n_chunks) * hb, RET_CHUNK, RET_DV), F32)],
        compiler_params=_params(2),
        name="retention_latent" if has_s0 else "retention_ctx",
    )(*args)


def _attn_kernel(*refs, lx, lk_ctx, bq, hb, carry):
    it = iter(refs)
    q_ref, k_ref, v_ref = next(it), next(it), next(it)
    kctx_ref, vctx_ref = (next(it), next(it)) if lk_ctx else (None, None)
    qn_ref, kn_ref, kctxn_ref = (next(it), next(it), next(it)) if carry else (None, None, None)
    lq1_ref, lk1_ref, lq2_ref, lk2_ref, sg_ref = (next(it) for _ in range(5))
    o_ref = next(it)
    vt, s_scr, m_scr, acc_scr = next(it), next(it), next(it), next(it)
    kctx_scr = next(it) if lk_ctx else None
    lk = lx + lk_ctx
    n_kc = lk // ATT_KC
    n_q = lx // bq

    def cols(j):
        return slice(j * LANES, (j + 1) * LANES)

    def chunk(c):
        return slice(c * ATT_KC, (c + 1) * ATT_KC)

    def keys(k_src, kctx_slot, j, c):
        lo = c * ATT_KC
        if lo < lx:
            return k_src[0, lo:lo + ATT_KC, cols(j)]
        return kctx_scr[kctx_slot, lo - lx:lo - lx + ATT_KC, :]

    if carry:
        kctx_scr[hb] = kctxn_ref[0, cols(0), :].T.astype(BF)
    for j in range(hb):
        if lk_ctx:
            kctx_scr[j] = kctx_ref[0, cols(j), :].T.astype(BF)
        for c in range(n_kc):
            lo = c * ATT_KC
            if lo < lx:
                vc = v_ref[0, lo:lo + ATT_KC, cols(j)].astype(F32)
            else:
                vc = vctx_ref[0, lo - lx:lo - lx + ATT_KC, cols(j)]
            vt[j, 0:LANES, chunk(c)] = vc.T.astype(BF)
        vt[j, LANES:ATT_VROWS, :] = jnp.ones((ATT_VROWS - LANES, lk), BF)

    def lam_term(a_ref, b_ref):
        return jnp.exp(jnp.sum(a_ref[...].astype(F32) * b_ref[...].astype(F32), axis=-1, keepdims=True))

    lam = lam_term(lq1_ref, lk1_ref) - lam_term(lq2_ref, lk2_ref) + LAM_INIT
    lane = lax.broadcasted_iota(jnp.int32, (bq, LANES), 1)

    def rows(t):
        return pl.ds(pl.multiple_of(t * bq, bq), bq)

    def scores(j, t, slot, following=False):
        q_src, k_src, kctx_slot = (qn_ref, kn_ref, hb) if following else (q_ref, k_ref, j)
        q = q_src[0, rows(t), cols(j)]
        zero = jnp.zeros_like(q)
        q_maps = (jnp.where(lane < DIFF_DH, q, zero), jnp.where(lane >= DIFF_DH, q, zero))
        m = [jnp.full((SUBLANES, bq), -jnp.inf, F32) for _ in range(2)]
        for c in range(n_kc):
            kk = keys(k_src, kctx_slot, j, c)
            for i in range(2):
                s = lax.dot_general(kk, q_maps[i], (((1,), (1,)), ((), ())),
                                    preferred_element_type=F32)
                s_scr[j, slot, i, chunk(c), :] = s
                m[i] = jnp.maximum(m[i], jnp.max(s.reshape(ATT_KC // SUBLANES, SUBLANES, bq), axis=0))
            yield
        for i in range(2):
            m_scr[j, slot, i] = jnp.broadcast_to(jnp.max(m[i], axis=0, keepdims=True), (SUBLANES, bq))

    def values(j, slot):
        m = [m_scr[j, slot, i][0:1, :] for i in range(2)]
        acc = [jnp.zeros((ATT_VROWS, bq), F32) for _ in range(2)]
        for c in range(n_kc):
            for i in range(2):
                e = jnp.exp2(s_scr[j, slot, i, chunk(c), :] - m[i]).astype(BF)
                acc[i] = acc[i] + jnp.dot(vt[j, :, chunk(c)], e, preferred_element_type=F32)
            yield
        for i in range(2):
            acc_scr[j, slot, i] = acc[i]

    def finish(j, t, slot):
        outs = [acc_scr[j, slot, i, 0:LANES, :] / acc_scr[j, slot, i, LANES:LANES + 1, :] for i in range(2)]
        ot = outs[0] - lam * outs[1]
        ms = jnp.mean(ot * ot, axis=0, keepdims=True)
        o = (ot * lax.rsqrt(ms + RMS_EPS)).T * sg_ref[...]
        o_ref[0, rows(t), cols(j)] = (o * (1.0 - LAM_INIT)).astype(BF)
        yield

    def chain(*streams):
        for g in streams:
            yield from g

    def run(*streams):
        live = list(streams)
        while live:
            live = [g for g in live if next(g, StopIteration) is not StopIteration]

    if n_q == 1:
        for j0 in range(0, hb, ATT_HEAD_STREAMS):
            run(*[chain(scores(j, 0, 0), values(j, 0), finish(j, 0, 0))
                  for j in range(j0, min(j0 + ATT_HEAD_STREAMS, hb))])
        return
    assert hb == 1 and n_q % 2 == 0 and n_q >= 4 and carry
    @pl.when((pl.program_id(0) == 0) & (pl.program_id(1) == 0))
    def _():
        run(scores(0, 0, 0))

    run(scores(0, 1, 1), values(0, 0))
    run(scores(0, 2, 0), chain(finish(0, 0, 0), values(0, 1)))

    def body(u, _):
        t = 2 * u
        run(scores(0, t + 1, 1), chain(finish(0, t - 1, 1), values(0, 0)))
        run(scores(0, t + 2, 0), chain(finish(0, t, 0), values(0, 1)))
        return 0

    lax.fori_loop(1, (n_q - 2) // 2, body, 0)
    run(scores(0, n_q - 1, 1), chain(finish(0, n_q - 3, 1), values(0, 0)))
    run(scores(0, 0, 0, following=True), chain(finish(0, n_q - 2, 0), values(0, 1)))
    run(finish(0, n_q - 1, 1))


def _attention(proj, kt_ctx, v_ctx, lam_params, subln_g, *, bq, hb):
    bx, lx, _ = proj.shape
    lk_ctx = 0 if kt_ctx is None else kt_ctx.shape[2]
    lk = lx + lk_ctx
    head_w = 2 * DIFF_DH
    width = hb * head_w
    q_off = 3 * SEG // width
    k_off = 4 * SEG // width
    v_off = 5 * SEG // width

    def heads(off):
        return pl.BlockSpec((1, lx, width), lambda b, h: (b, 0, off + h))

    in_specs = [heads(q_off), heads(k_off), heads(v_off)]
    args = [proj, proj, proj]
    scratch = [pltpu.VMEM((hb, ATT_VROWS, lk), BF),
               pltpu.VMEM((hb, 2, 2, lk, bq), F32), pltpu.VMEM((hb, 2, 2, SUBLANES, bq), F32),
               pltpu.VMEM((hb, 2, 2, ATT_VROWS, bq), F32)]
    carry = lx // bq > 1
    if lk_ctx:
        assert lk_ctx % ATT_KC == 0
        in_specs += [pl.BlockSpec((1, width, lk_ctx), lambda b, h: (b, h, 0)),
                     pl.BlockSpec((1, lk_ctx, width), lambda b, h: (b, 0, h))]
        args += [kt_ctx, v_ctx]
        scratch.append(pltpu.VMEM((hb + carry, lk_ctx, head_w), BF))
    if carry:
        assert hb == 1 and lk_ctx
        last = bx * DIFF_HEADS - 1

        def following(b, h):
            g = jnp.minimum(b * DIFF_HEADS + h + 1, last)
            return g // DIFF_HEADS, g % DIFF_HEADS

        in_specs += [pl.BlockSpec((1, lx, width), lambda b, h: (following(b, h)[0], 0, q_off + following(b, h)[1])),
                     pl.BlockSpec((1, lx, width), lambda b, h: (following(b, h)[0], 0, k_off + following(b, h)[1])),
                     pl.BlockSpec((1, width, lk_ctx), lambda b, h: (following(b, h)[0], following(b, h)[1], 0))]
        args += [proj, proj, kt_ctx]
    small = pl.BlockSpec((1, DIFF_DH), lambda b, h: (0, 0))
    in_specs += [small] * 4 + [pl.BlockSpec((1, head_w), lambda b, h: (0, 0))]
    args += list(lam_params) + [subln_g]
    return pl.pallas_call(
        functools.partial(_attn_kernel, lx=lx, lk_ctx=lk_ctx, bq=bq, hb=hb, carry=carry),
        out_shape=jax.ShapeDtypeStruct((bx, lx, DIFF_HEADS * head_w), BF),
        grid=(bx, DIFF_HEADS // hb),
        in_specs=in_specs,
        out_specs=heads(0),
        scratch_shapes=scratch,
        compiler_params=_params(2),
        name="diff_attn_latent" if lk_ctx else "diff_attn_ctx",
    )(*args)


def _post_kernel(x_ref, yr_ref, od_ref, sh1_ref, sc1_ref, gate1_ref, sh2_ref, sc2_ref, gate2_ref,
                 n1_ref, bg_ref, n2_ref, fn_ref, wgt_ref, wr_ref, wd_ref, wo_ref, wg_ref, wu_ref, wdn_ref, o_ref,
                 *, tm):
    def rows_stream(rs):
        x = x_ref[0, rs, :]
        ms1 = jnp.mean(x * x, axis=-1, keepdims=True)
        h1 = x * lax.rsqrt(ms1 + RMS_EPS) * n1_ref[...]
        h1 = (h1 * (1.0 + sc1_ref[0]) + sh1_ref[0]).astype(BF)
        gates = jnp.dot(h1, wgt_ref[...], preferred_element_type=F32)
        br = jnp.dot(yr_ref[0, rs, :], wr_ref[...], preferred_element_type=F32)
        bd = jnp.dot(od_ref[0, rs, :], wd_ref[...], preferred_element_type=F32)
        yield
        gates = jax.nn.sigmoid(gates + bg_ref[...])
        merged = (gates[:, 0:SEG] * br + gates[:, SEG:GATE_COLS] * bd).astype(BF)
        mix = jnp.dot(merged, wo_ref[...], preferred_element_type=F32)
        yield
        x1 = x + gate1_ref[0] * mix
        ms = jnp.mean(x1 * x1, axis=-1, keepdims=True)
        h2 = x1 * lax.rsqrt(ms + RMS_EPS) * n2_ref[...]
        h2 = (h2 * (1.0 + sc2_ref[0]) + sh2_ref[0]).astype(BF)
        ffn = None
        for lo, hi in FFN_SPLITS:
            g = jnp.dot(h2, wg_ref[:, lo:hi], preferred_element_type=F32)
            u = jnp.dot(h2, wu_ref[:, lo:hi], preferred_element_type=F32)
            yield
            act = (g * jax.nn.sigmoid(g) * u).astype(BF)
            part = jnp.dot(act, wdn_ref[lo:hi, :], preferred_element_type=F32)
            ffn = part if ffn is None else ffn + part
            yield
        x2 = x1 + gate2_ref[0] * ffn
        ms2 = jnp.mean(x2 * x2, axis=-1, keepdims=True)
        o_ref[0, rs, :] = x2 * lax.rsqrt(ms2 + RMS_EPS) * fn_ref[...]

    def delayed(stream, stages):
        for _ in range(stages):
            yield
        yield from stream

    n_groups = tm // POST_ROWS
    live = [delayed(rows_stream(slice(r * POST_ROWS, (r + 1) * POST_ROWS)), 2 * r) for r in range(n_groups)]
    while live:
        live = [g for g in live if next(g, StopIteration) is not StopIteration]


def _post(x, y_r, o_d, mod3, norm1_g, b_gate, norm2_g, final_g, w_in_bf, weights, *, latent, tm):
    bx, lx, _ = x.shape
    row = (lambda b: b) if latent else (lambda b: CTX_ROW)

    def tok(width, col):
        return pl.BlockSpec((1, tm, width), lambda b, m: (b, m, col))

    def mod_spec(k):
        return pl.BlockSpec((1, 1, D_MODEL), lambda b, m: (row(b), 0, k))

    def resident(shape):
        return pl.BlockSpec(shape, lambda b, m: (0, 0), pipeline_mode=pl.Buffered(1))

    vec = pl.BlockSpec((1, D_MODEL), lambda b, m: (0, 0))
    gate_block = MIXER_COLS // GATE_COLS
    in_specs = [tok(D_MODEL, 0), tok(D_MODEL, 0), tok(D_MODEL, 0),
                mod_spec(0), mod_spec(1), mod_spec(2), mod_spec(3), mod_spec(4), mod_spec(5),
                vec, resident(b_gate.shape), vec, vec,
                pl.BlockSpec((D_MODEL, GATE_COLS), lambda b, m: (0, gate_block), pipeline_mode=pl.Buffered(1))]
    in_specs += [resident(w.shape) for w in weights]
    return pl.pallas_call(
        functools.partial(_post_kernel, tm=tm),
        out_shape=jax.ShapeDtypeStruct((bx, lx, D_MODEL), F32),
        grid=(bx, lx // tm),
        in_specs=in_specs,
        out_specs=tok(D_MODEL, 0),
        compiler_params=_params(2),
        name="post_latent" if latent else "post_ctx",
    )(x, y_r, o_d, mod3, mod3, mod3, mod3, mod3, mod3, norm1_g, b_gate, norm2_g, final_g, w_in_bf, *weights)


def kernel(x_prompt, x_sample, state_ret_fwd, state_ret_bwd, cache_diff_k, cache_diff_v, c, c_ctx,
           norm1_g, norm2_g, w_ada, b_ada, w_in, b_gate, ret_decay_fwd, ret_decay_bwd, ret_gn_g,
           w_ret_out, diff_lambda_q1, diff_lambda_k1, diff_lambda_q2, diff_lambda_k2, diff_subln_g,
           w_diff_out, w_o, w_ffn_gate, w_ffn_up, w_ffn_down, final_norm_g):
    batch, seq, _ = x_prompt.shape
    dec_batch, dec_seq, _ = x_sample.shape
    past_len = cache_diff_k.shape[2]
    layer = 0

    cvec = jnp.zeros((MOD_ROWS, D_MODEL), F32).at[:dec_batch].set(c).at[CTX_ROW].set(c_ctx)
    mod = _modulation(cvec, w_ada[layer], b_ada[layer][None, :])
    mod3 = mod.reshape(MOD_ROWS, 1, 6 * D_MODEL)

    w_in_bf = w_in[layer].astype(BF)
    post_w = tuple(w[layer].astype(BF) for w in (w_ret_out, w_diff_out, w_o, w_ffn_gate, w_ffn_up, w_ffn_down))
    n1 = norm1_g[layer][None, :]
    n2 = norm2_g[layer][None, :]
    fn = final_norm_g[None, :]
    bg = b_gate[layer][None, :]
    dec_f = ret_decay_fwd[layer][None, :]
    dec_b = ret_decay_bwd[layer][None, :]
    gn = ret_gn_g[layer][None, :]
    lam_params = tuple(p[layer][None, :] for p in (diff_lambda_q1, diff_lambda_k1, diff_lambda_q2, diff_lambda_k2))
    subln = diff_subln_g[layer][None, :]

    xp_flat = x_prompt.reshape(1, batch * seq, D_MODEL)
    assert seq == PROJ_ROWS
    proj_c, kt_new, v_new = _inproj(xp_flat, mod3, n1, w_in_bf, None, latent=False, tm=TOKEN_TILE)
    proj_c_b = proj_c.reshape(batch, seq, MIXER_COLS)
    yr_c, s_f, s_b = _retention(proj_c_b, dec_f, dec_b, gn, None, None, emit_state=True, hb=RET_HEADS)
    od_c = _attention(proj_c_b, None, None, lam_params, subln, bq=seq, hb=DIFF_HEADS)
    y_prompt = _post(xp_flat, yr_c.reshape(1, batch * seq, -1), od_c.reshape(1, batch * seq, -1),
                     mod3, n1, bg, n2, fn, w_in_bf, post_w, latent=False, tm=TOKEN_TILE).reshape(batch, seq, D_MODEL)

    rope = _rope_tables(dec_seq)
    proj_l = _inproj(x_sample, mod3, n1, w_in_bf, rope, latent=True, tm=TOKEN_TILE)
    yr_l = _retention(proj_l, dec_f, dec_b, gn, state_ret_fwd, state_ret_bwd, emit_state=False, hb=RET_HB_LATENT)
    kt_ctx = jnp.transpose(cache_diff_k[:, layer], (0, 2, 3, 4, 1)).reshape(dec_batch, -1, past_len)
    v_ctx = cache_diff_v[:, layer].reshape(dec_batch, past_len, DIFF_HEADS * 2 * DIFF_DH)
    od_l = _attention(proj_l, kt_ctx, v_ctx, lam_params, subln, bq=ATT_BQ, hb=1)
    y_sample = _post(x_sample, yr_l, od_l, mod3, n1, bg, n2, fn, w_in_bf, post_w, latent=True, tm=TOKEN_TILE)

    new_diff_k = jnp.transpose(kt_new.reshape(batch, DIFF_HEADS, 2, DIFF_DH, seq), (0, 4, 1, 2, 3))[:, None]
    new_diff_v = v_new.reshape(batch, 1, seq, DIFF_HEADS, 2 * DIFF_DH)
    return (y_prompt, y_sample, s_f, s_b, new_diff_k, new_diff_v)
```
